```python
import jax, jax.numpy as jnp
from jax import lax
import numpy as np

D_MODEL = 1024
BATCH = 8
SEQ = 2048
DEPTH = 4
DEC_BATCH = 128
DEC_SEQ = 8
PAST_LEN = 16384
PAGE_SIZE = 128

MIX_WIDTH = D_MODEL
GLA_HEADS = 4
GLA_WIDTH = MIX_WIDTH // 2
GLA_DV = GLA_WIDTH // GLA_HEADS
GLA_DK = GLA_DV // 2
GLA_KW = GLA_HEADS * GLA_DK
GLA_RANK = 16
GLA_GATE_NORM = 16.0
GDN_HEADS = 4
GDN_WIDTH = MIX_WIDTH - GLA_WIDTH
GDN_DK = GDN_WIDTH // GDN_HEADS
GDN_DV = GDN_DK
CONV_W = 4
CHUNK = 64
PEER_HEADS = 8
PEER_QDIM = 256
PEER_HALF = PEER_QDIM // 2
N_KEYS = 128
N_EXPERTS = N_KEYS * N_KEYS
PEER_TOPK = 16
PEER_BLOCK = 256
PLE_DIM = 256
EPS = 1e-6
SPLIT_SIZES = (GLA_KW, GLA_KW, GLA_WIDTH, GLA_WIDTH, GLA_RANK, 3 * GDN_WIDTH, GDN_WIDTH, GDN_HEADS, GDN_HEADS)
IN_WIDTH = sum(SPLIT_SIZES)

kernel_name = 'hybrid_gla_gdn_peer_step'


def _rmsnorm(x, g):
    xf = x.astype(jnp.float32)
    y = xf * lax.rsqrt(jnp.mean(xf * xf, axis=-1, keepdims=True) + EPS)
    return (y * g.astype(jnp.float32)).astype(x.dtype)


def _l2norm(x):
    return x * lax.rsqrt(jnp.sum(x * x, axis=-1, keepdims=True) + EPS)


def _split_cols(x, sizes):
    out, off = [], 0
    for s in sizes:
        out.append(x[..., off:off + s])
        off += s
    return out


def _chunk_len(L):
    return CHUNK if L >= CHUNK else L


def _to_chunks(x, C):
    B, L = x.shape[0], x.shape[1]
    n_pad = (-L) % C
    x = jnp.pad(x, [(0, 0), (0, n_pad)] + [(0, 0)] * (x.ndim - 2))
    x = x.reshape((B, (L + n_pad) // C, C) + x.shape[2:])
    return jnp.moveaxis(x, 1, 0)


def _from_chunks(y, L):
    nc, B, C = y.shape[0], y.shape[1], y.shape[2]
    y = jnp.moveaxis(y, 0, 1).reshape((B, nc * C) + y.shape[3:])
    return y[:, :L]


def _gla_recurrence(q, k, v, gk, s0):
    L = q.shape[1]
    C = _chunk_len(L)
    incl = jnp.tril(jnp.ones((C, C), dtype=bool))[None, :, :, None, None]

    def step(S, inp):
        qc, kc, vc, gc = inp
        b = jnp.cumsum(gc, axis=1)
        rel = b[:, :, None] - b[:, None, :]
        decay = jnp.where(incl, jnp.exp(jnp.where(incl, rel, 0.0)), 0.0)
        att = jnp.einsum('bthk,bshk,btshk->bhts', qc, kc, decay)
        o = jnp.einsum('bhts,bshv->bthv', att, vc) + jnp.einsum('bthk,bhkv->bthv', qc * jnp.exp(b), S)
        b_last = b[:, -1]
        S = jnp.exp(b_last)[..., None] * S + jnp.einsum('bshk,bshv->bhkv', kc * jnp.exp(b_last[:, None] - b), vc)
        return S, o

    S, o = lax.scan(step, s0, (_to_chunks(q, C), _to_chunks(k, C), _to_chunks(v, C), _to_chunks(gk, C)))
    return _from_chunks(o, L), S


def _gdn_recurrence(q, k, v, g, beta, s0):
    L = q.shape[1]
    C = _chunk_len(L)
    incl = jnp.tril(jnp.ones((C, C), dtype=bool))
    strict = jnp.tril(jnp.ones((C, C), dtype=bool), k=-1)
    eye = jnp.eye(C, dtype=jnp.float32)

    def step(S, inp):
        qc, kc, vc, gc, bc = inp
        b = jnp.cumsum(gc, axis=1)
        bh = jnp.moveaxis(b, 1, 2)
        rel = bh[..., :, None] - bh[..., None, :]
        decay = jnp.where(incl, jnp.exp(jnp.where(incl, rel, 0.0)), 0.0)
        kb = kc * bc[..., None]
        lower = jnp.where(strict, jnp.einsum('bthk,bshk->bhts', kb, kc) * decay, 0.0)
        rhs = jnp.concatenate([vc * bc[..., None], kb * jnp.exp(b)[..., None]], axis=-1)
        rhs = jnp.moveaxis(rhs, 1, 2)
        sol = lax.linalg.triangular_solve(eye + lower, rhs, left_side=True, lower=True, unit_diagonal=True)
        u, w = sol[..., :GDN_DV], sol[..., GDN_DV:]
        v_new = u - jnp.einsum('bhtk,bhkv->bhtv', w, S)
        qk = jnp.einsum('bthk,bshk->bhts', qc, kc) * decay
        o = jnp.einsum('bthk,bhkv->bthv', qc * jnp.exp(b)[..., None], S) + jnp.einsum('bhts,bhsv->bthv', qk, v_new)
        b_last = b[:, -1]
        S = jnp.exp(b_last)[..., None, None] * S + jnp.einsum('bshk,bhsv->bhkv', kc * jnp.exp(b_last[:, None] - b)[..., None], v_new)
        return S, o

    xs = (_to_chunks(q, C), _to_chunks(k, C), _to_chunks(v, C), _to_chunks(g, C), _to_chunks(beta, C))
    S, o = lax.scan(step, s0, xs)
    return _from_chunks(o, L), S


def _mixer(a, w_in, w_gla_up, b_gla_gate, gla_norm, conv_w, gdn_a_log, gdn_dt_bias, gdn_norm, w_out, s_gla, s_gdn, conv_buf):
    f32 = jnp.float32
    B, L, _ = a.shape
    q1, k1, v1, r1, lr, qkv2, z2, b2, a2 = _split_cols(a @ w_in, SPLIT_SIZES)
    gk = jax.nn.log_sigmoid((lr @ w_gla_up + b_gla_gate).astype(f32)) / GLA_GATE_NORM
    q1 = q1.astype(f32).reshape(B, L, GLA_HEADS, GLA_DK) * (GLA_DK ** -0.5)
    k1 = k1.astype(f32).reshape(B, L, GLA_HEADS, GLA_DK)
    v1 = v1.astype(f32).reshape(B, L, GLA_HEADS, GLA_DV)
    gk = gk.reshape(B, L, GLA_HEADS, GLA_DK)
    o1, s_gla_new = _gla_recurrence(q1, k1, v1, gk, s_gla.astype(f32))
    o1 = _rmsnorm(o1, gla_norm).reshape(B, L, GLA_WIDTH) * jax.nn.silu(r1.astype(f32))
    xx = jnp.concatenate([conv_buf.astype(qkv2.dtype), qkv2], axis=1)
    conv = xx[:, 0:L] * conv_w[0]
    for j in range(1, CONV_W):
        conv = conv + xx[:, j:j + L] * conv_w[j]
    conv_new = xx[:, L:]
    q2, k2, v2 = _split_cols(jax.nn.silu(conv.astype(f32)), (GDN_WIDTH, GDN_WIDTH, GDN_WIDTH))
    q2 = _l2norm(q2.reshape(B, L, GDN_HEADS, GDN_DK)) * (GDN_DK ** -0.5)
    k2 = _l2norm(k2.reshape(B, L, GDN_HEADS, GDN_DK))
    v2 = v2.reshape(B, L, GDN_HEADS, GDN_DV)
    beta = jax.nn.sigmoid(b2.astype(f32))
    g = -jnp.exp(gdn_a_log.astype(f32)) * jax.nn.softplus(a2.astype(f32) + gdn_dt_bias.astype(f32))
    o2, s_gdn_new = _gdn_recurrence(q2, k2, v2, g, beta, s_gdn.astype(f32))
    o2 = (_rmsnorm(o2, gdn_norm) * jax.nn.silu(z2.astype(f32).reshape(B, L, GDN_HEADS, GDN_DV))).reshape(B, L, GDN_WIDTH)
    o = jnp.concatenate([o1, o2], axis=-1).astype(a.dtype) @ w_out
    return o, s_gla_new, s_gdn_new, conv_new


def _peer(a, w_pq, sub_keys, u_tab, v_tab):
    B, L, D = a.shape
    n = B * L
    nb = -(-n // PEER_BLOCK)
    t = jnp.pad(a.reshape(n, D), ((0, nb * PEER_BLOCK - n), (0, 0)))
    blocks = t.reshape(nb, PEER_BLOCK, D)
    keys = sub_keys.astype(jnp.float32)

    def block_fn(xb):
        q = (xb @ w_pq).astype(jnp.float32).reshape(PEER_BLOCK, PEER_HEADS, 2, PEER_HALF)
        s = jnp.einsum('thcd,hcnd->thcn', q, keys)
        sv, si = lax.top_k(s, PEER_TOPK)
        cand = sv[:, :, 0, :, None] + sv[:, :, 1, None, :]
        cand_idx = si[:, :, 0, :, None] * N_KEYS + si[:, :, 1, None, :]
        cv, ci = lax.top_k(cand.reshape(PEER_BLOCK, PEER_HEADS, PEER_TOPK * PEER_TOPK), PEER_TOPK)
        idx = jnp.take_along_axis(cand_idx.reshape(PEER_BLOCK, PEER_HEADS, PEER_TOPK * PEER_TOPK), ci, axis=-1)
        gate = jax.nn.softmax(cv, axis=-1)
        u = jnp.take(u_tab, idx, axis=0)
        hid = jax.nn.gelu(jnp.einsum('td,thkd->thk', xb, u).astype(jnp.float32), approximate=False)
        vv = jnp.take(v_tab, idx, axis=0)
        return jnp.einsum('thk,thkd->td', (gate * hid).astype(xb.dtype), vv)

    out = lax.map(block_fn, blocks).reshape(nb * PEER_BLOCK, D)[:n]
    return out.reshape(B, L, D)


def _trunk(x, p, s_gla, s_gdn, s_conv, params):
    (norm_mix, w_in, w_gla_up, b_gla_gate, gla_norm, conv_w, gdn_a_log, gdn_dt_bias, gdn_norm, w_out,
     norm_ffn, w_peer_q, peer_sub_keys, peer_u, peer_v, norm_ple, w_ple_gate, b_ple_gate, w_ple_proj, norm_final) = params
    h = x
    new_gla, new_gdn, new_conv = [], [], []
    for i in range(DEPTH):
        mix, sg, sd, sc = _mixer(_rmsnorm(h, norm_mix[i]), w_in[i], w_gla_up[i], b_gla_gate[i], gla_norm[i], conv_w[i],
                                 gdn_a_log[i], gdn_dt_bias[i], gdn_norm[i], w_out[i], s_gla[i], s_gdn[i], s_conv[i])
        h = h + mix
        h = h + _peer(_rmsnorm(h, norm_ffn[i]), w_peer_q[i], peer_sub_keys[i], peer_u[i], peer_v[i])
        gate = jax.nn.sigmoid(_rmsnorm(h, norm_ple[i]) @ w_ple_gate[i] + b_ple_gate[i])
        h = h + gate * (p[i] @ w_ple_proj[i])
        new_gla.append(sg)
        new_gdn.append(sd)
        new_conv.append(sc)
    y = _rmsnorm(h, norm_final)
    return y, jnp.stack(new_gla), jnp.stack(new_gdn), jnp.stack(new_conv)


def setup_inputs(seed: int = 0) -> dict:
    key = jax.random.key(seed)
    ks = jax.random.split(key, 32)
    f32 = jnp.float32

    def nrm(k, shape, scale):
        return jax.random.normal(k, shape, f32) * scale

    def gain(k, shape):
        return 1.0 + 0.02 * jax.random.normal(k, shape, f32)

    return {
        'x_prompt': nrm(ks[0], (BATCH, SEQ, D_MODEL), 1.0),
        'x_sample': nrm(ks[1], (DEC_BATCH, DEC_SEQ, D_MODEL), 1.0),
        'state_gla': nrm(ks[2], (DEPTH, DEC_BATCH, GLA_HEADS, GLA_DK, GLA_DV), 0.1),
        'state_gdn': nrm(ks[3], (DEPTH, DEC_BATCH, GDN_HEADS, GDN_DK, GDN_DV), 0.1),
        'state_conv': nrm(ks[4], (DEPTH, DEC_BATCH, CONV_W - 1, 3 * GDN_WIDTH), 1.0),
        'p_prompt': nrm(ks[5], (DEPTH, BATCH, SEQ, PLE_DIM), 1.0),
        'p_sample': nrm(ks[6], (DEPTH, DEC_BATCH, DEC_SEQ, PLE_DIM), 1.0),
        'norm_mix': gain(ks[7], (DEPTH, D_MODEL)),
        'w_in': nrm(ks[8], (DEPTH, D_MODEL, IN_WIDTH), D_MODEL ** -0.5),
        'w_gla_up': nrm(ks[9], (DEPTH, GLA_RANK, GLA_KW), GLA_RANK ** -0.5),
        'b_gla_gate': nrm(ks[10], (DEPTH, GLA_KW), 0.1),
        'gla_norm': gain(ks[11], (DEPTH, GLA_DV)),
        'conv_w': nrm(ks[12], (DEPTH, CONV_W, 3 * GDN_WIDTH), CONV_W ** -0.5),
        'gdn_a_log': jnp.log(jax.random.uniform(ks[13], (DEPTH, GDN_HEADS), f32, 1.0, 16.0)),
        'gdn_dt_bias': nrm(ks[14], (DEPTH, GDN_HEADS), 0.1),
        'gdn_norm': gain(ks[15], (DEPTH, GDN_DV)),
        'w_out': nrm(ks[16], (DEPTH, MIX_WIDTH, D_MODEL), 0.5 * MIX_WIDTH ** -0.5),
        'norm_ffn': gain(ks[17], (DEPTH, D_MODEL)),
        'w_peer_q': nrm(ks[18], (DEPTH, D_MODEL, PEER_HEADS * PEER_QDIM), D_MODEL ** -0.5),
        'peer_sub_keys': nrm(ks[19], (DEPTH, PEER_HEADS, 2, N_KEYS, PEER_HALF), PEER_HALF ** -0.5),
        'peer_u': nrm(ks[20], (DEPTH, N_EXPERTS, D_MODEL), D_MODEL ** -0.5),
        'peer_v': nrm(ks[21], (DEPTH, N_EXPERTS, D_MODEL), 0.5 * PEER_HEADS ** -0.5),
        'norm_ple': gain(ks[22], (DEPTH, D_MODEL)),
        'w_ple_gate': nrm(ks[23], (DEPTH, D_MODEL, D_MODEL), D_MODEL ** -0.5),
        'b_ple_gate': nrm(ks[24], (DEPTH, D_MODEL), 0.1),
        'w_ple_proj': nrm(ks[25], (DEPTH, PLE_DIM, D_MODEL), 0.5 * PLE_DIM ** -0.5),
        'norm_final': gain(ks[26], (D_MODEL,)),
    }


def reference(x_prompt, x_sample, state_gla, state_gdn, state_conv, p_prompt, p_sample,
              norm_mix, w_in, w_gla_up, b_gla_gate, gla_norm, conv_w, gdn_a_log, gdn_dt_bias, gdn_norm, w_out,
              norm_ffn, w_peer_q, peer_sub_keys, peer_u, peer_v, norm_ple, w_ple_gate, b_ple_gate, w_ple_proj, norm_final):
    params = (norm_mix, w_in, w_gla_up, b_gla_gate, gla_norm, conv_w, gdn_a_log, gdn_dt_bias, gdn_norm, w_out,
              norm_ffn, w_peer_q, peer_sub_keys, peer_u, peer_v, norm_ple, w_ple_gate, b_ple_gate, w_ple_proj, norm_final)
    B = x_prompt.shape[0]
    gla0 = jnp.zeros((DEPTH, B, GLA_HEADS, GLA_DK, GLA_DV), jnp.float32)
    gdn0 = jnp.zeros((DEPTH, B, GDN_HEADS, GDN_DK, GDN_DV), jnp.float32)
    conv0 = jnp.zeros((DEPTH, B, CONV_W - 1, 3 * GDN_WIDTH), x_prompt.dtype)
    y_prompt, gla_p, gdn_p, conv_p = _trunk(x_prompt, p_prompt, gla0, gdn0, conv0, params)
    y_sample, gla_s, gdn_s, conv_s = _trunk(x_sample, p_sample, state_gla, state_gdn, state_conv, params)
    return (y_prompt, y_sample, gla_p, gdn_p, conv_p, gla_s, gdn_s, conv_s)
```

```python
import functools

import jax
import jax.numpy as jnp
from jax import lax
from jax.experimental import pallas as pl
from jax.experimental.pallas import tpu as pltpu

F32 = jnp.float32
BF16 = jnp.bfloat16
HI = lax.Precision.HIGHEST

D_MODEL = 1024
DEPTH = 4
GLA_HEADS = 4
GLA_WIDTH = 512
GLA_DV = 128
GLA_DK = 64
GLA_KW = 256
GLA_RANK = 16
GLA_GATE_NORM = 16.0
GDN_HEADS = 4
GDN_WIDTH = 512
GDN_DK = 128
GDN_DV = 128
CONV_W = 4
CHUNK = 64
PEER_HEADS = 8
PEER_HALF = 128
N_KEYS = 128
N_EXPERTS = N_KEYS * N_KEYS
PEER_TOPK = 16
PLE_DIM = 256
EPS = 1e-6

LANES = 128
SUBLANES = 8
GLA_IN = 2 * GLA_KW + 2 * GLA_WIDTH + LANES
GDN_IN = 3 * GDN_WIDTH + GDN_WIDTH + LANES
GLA_SUB = 16
VMEM_LIMIT = 56 * 1024 * 1024


def _tile(n, pref):
    t = pref
    while n % t:
        t -= LANES
    return t


def _cparams(sem):
    return pltpu.CompilerParams(dimension_semantics=sem, vmem_limit_bytes=VMEM_LIMIT)


def _dot(a, b, prec=None):
    return jnp.dot(a, b, precision=prec, preferred_element_type=F32)


def _dot_nt(a, b, prec=None):
    return lax.dot_general(a, b, (((1,), (1,)), ((), ())), precision=prec, preferred_element_type=F32)


def _dot_tn(a, b, prec=None):
    return lax.dot_general(a, b, (((0,), (0,)), ((), ())), precision=prec, preferred_element_type=F32)


def _rms(x, g):
    return x * lax.rsqrt(jnp.mean(x * x, axis=-1, keepdims=True) + EPS) * g


def _softplus(x):
    return jnp.maximum(x, 0.0) + jnp.log1p(jnp.exp(-jnp.abs(x)))


def _silu(x):
    return x * jax.nn.sigmoid(x)


def _in_proj_kernel(h_ref, g_ref, w_ref, gla_ref, gdn_ref):
    xn = _rms(h_ref[...], g_ref[...]).astype(BF16)
    gla_ref[...] = _dot(xn, w_ref[:, :GLA_IN])
    gdn_ref[...] = _dot(xn, w_ref[:, GLA_IN:])


def _in_proj(h, g, w):
    n = h.shape[0]
    tm = _tile(n, 256)
    return pl.pallas_call(
        _in_proj_kernel,
        grid=(n // tm,),
        in_specs=[pl.BlockSpec((tm, D_MODEL), lambda i: (i, 0)),
                  pl.BlockSpec((1, D_MODEL), lambda i: (0, 0)),
                  pl.BlockSpec((D_MODEL, GLA_IN + GDN_IN), lambda i: (0, 0))],
        out_specs=[pl.BlockSpec((tm, GLA_IN), lambda i: (i, 0)),
                   pl.BlockSpec((tm, GDN_IN), lambda i: (i, 0))],
        out_shape=[jax.ShapeDtypeStruct((n, GLA_IN), F32), jax.ShapeDtypeStruct((n, GDN_IN), F32)],
        compiler_params=_cparams(("parallel",)),
        name="in_proj",
    )(h, g, w)


def _gla_kernel(x_ref, s0_ref, wup_ref, bg_ref, gn_ref, o_ref, s_ref, *, C, SB, bb):
    c = pl.program_id(1)

    @pl.when(c == 0)
    def _():
        s_ref[...] = s0_ref[...]

    nb = C // SB
    lane = lax.broadcasted_iota(jnp.int32, (1, LANES), 1)
    hmask = (lane < GLA_DK, lane >= GLA_DK)
    tri = (lax.broadcasted_iota(jnp.int32, (C, C), 0) >= lax.broadcasted_iota(jnp.int32, (C, C), 1)).astype(F32)
    t_sb = lax.broadcasted_iota(jnp.int32, (SB, 1), 0)
    col_c = lax.broadcasted_iota(jnp.int32, (SB, C), 1)
    eye = lax.broadcasted_iota(jnp.int32, (LANES, LANES), 0) == lax.broadcasted_iota(jnp.int32, (LANES, LANES), 1)
    row_l = lax.broadcasted_iota(jnp.int32, (LANES, 1), 0)

    for s in range(bb):
        rows = pl.ds(s * C, C)
        z = _dot(x_ref[rows, 2 * GLA_KW + 2 * GLA_WIDTH:], wup_ref[...], HI) + bg_ref[...]
        gk = (jnp.minimum(z, 0.0) - jnp.log1p(jnp.exp(-jnp.abs(z)))) * (1.0 / GLA_GATE_NORM)
        b_all = _dot(tri, gk, HI)
        for p in range(2):
            q = x_ref[rows, p * LANES:(p + 1) * LANES] * (GLA_DK ** -0.5)
            k = x_ref[rows, GLA_KW + p * LANES:GLA_KW + (p + 1) * LANES]
            b = b_all[:, p * LANES:(p + 1) * LANES]
            vs = [x_ref[rows, 2 * GLA_KW + (2 * p + h) * GLA_DV:2 * GLA_KW + (2 * p + h + 1) * GLA_DV]
                  for h in range(2)]
            S = s_ref[s, p]

            o_blocks = ([], [])
            for i in range(nb):
                bi, qi, ki = b[i * SB:(i + 1) * SB], q[i * SB:(i + 1) * SB], k[i * SB:(i + 1) * SB]
                att_d = [jnp.zeros((SB, C), F32), jnp.zeros((SB, C), F32)]
                for j in range(SB):
                    e = jnp.exp(jnp.where(t_sb >= j, bi - bi[j:j + 1], 0.0))
                    term = qi * ki[j:j + 1] * e
                    for h in range(2):
                        col = jnp.sum(jnp.where(hmask[h], term, 0.0), axis=-1, keepdims=True)
                        att_d[h] = jnp.where(col_c == i * SB + j, jnp.where(t_sb >= j, col, 0.0), att_d[h])
                if i > 0:
                    ri = b[i * SB - 1:i * SB]
                    qd = qi * jnp.exp(bi - ri)
                    kd = k * jnp.exp(jnp.minimum(ri - b, 0.0))
                for h in range(2):
                    att = att_d[h]
                    if i > 0:
                        att = jnp.where(col_c < i * SB, _dot_nt(jnp.where(hmask[h], qd, 0.0), kd), att)
                    o_blocks[h].append(_dot(att, vs[h]))

            qe = q * jnp.exp(b)
            b_last = b[C - 1:C]
            kt = k * jnp.exp(b_last - b)
            upd = _dot_tn(kt, jnp.concatenate(vs, axis=1))
            dec = jnp.where(eye, jnp.broadcast_to(jnp.exp(b_last), (LANES, LANES)), 0.0)
            s_ref[s, p] = _dot(dec, S, HI) + jnp.where(row_l < GLA_DK, upd[:, :GLA_DV], upd[:, GLA_DV:])
            for h in range(2):
                o_intra = o_blocks[h][0] if nb == 1 else jnp.concatenate(o_blocks[h], axis=0)
                o = o_intra + _dot(jnp.where(hmask[h], qe, 0.0), S)
                hh = 2 * p + h
                r = x_ref[rows, 2 * GLA_KW + GLA_WIDTH + hh * GLA_DV:2 * GLA_KW + GLA_WIDTH + (hh + 1) * GLA_DV]
                o_ref[rows, hh * GLA_DV:(hh + 1) * GLA_DV] = _rms(o, gn_ref[...]) * _silu(r)


def _gla(x, s0, wup, bg, gn, *, B, L, row0, bb):
    C = min(CHUNK, L)
    SB = min(GLA_SUB, C)
    nc = L // C
    rb = bb * C
    blk0 = row0 // rb
    kern = functools.partial(_gla_kernel, C=C, SB=SB, bb=bb)
    return pl.pallas_call(
        kern,
        grid=(B // bb, nc),
        in_specs=[pl.BlockSpec((rb, GLA_IN), lambda b, c: (blk0 + b * nc + c, 0)),
                  pl.BlockSpec((bb, 2, LANES, GLA_DV), lambda b, c: (b, 0, 0, 0)),
                  pl.BlockSpec((LANES, GLA_KW), lambda b, c: (0, 0)),
                  pl.BlockSpec((1, GLA_KW), lambda b, c: (0, 0)),
                  pl.BlockSpec((1, GLA_DV), lambda b, c: (0, 0))],
        out_specs=[pl.BlockSpec((rb, GLA_WIDTH), lambda b, c: (b * nc + c, 0)),
                   pl.BlockSpec((bb, 2, LANES, GLA_DV), lambda b, c: (b, 0, 0, 0))],
        out_shape=[jax.ShapeDtypeStruct((B * L, GLA_WIDTH), F32),
                   jax.ShapeDtypeStruct((B, 2, LANES, GLA_DV), F32)],
        compiler_params=_cparams(("parallel", "arbitrary")),
        name="gla",
    )(x, s0, wup, bg, gn)


def _gdn_kernel(x_ref, s0_ref, cb_ref, cw_ref, alog_ref, dt_ref, gn_ref, o_ref, s_ref, cn_ref, xbuf, *, C, bb):
    c = pl.program_id(1)
    W = 3 * GDN_WIDTH
    P0 = SUBLANES - (CONV_W - 1)

    @pl.when(c == 0)
    def _():
        s_ref[...] = s0_ref[...]
        for s in range(bb):
            xbuf[s, P0:SUBLANES, :] = cb_ref[s]

    ri = lax.broadcasted_iota(jnp.int32, (C, C), 0)
    ci = lax.broadcasted_iota(jnp.int32, (C, C), 1)
    incl = ri >= ci
    strict = ri > ci
    tri = incl.astype(F32)
    eye = (ri == ci).astype(F32)
    ones = jnp.ones((C, C), F32)
    n_sq = max(C.bit_length() - 2, 0)

    for s in range(bb):
        rows = pl.ds(s * C, C)
        xbuf[s, SUBLANES:SUBLANES + C, :] = x_ref[rows, :W]
        conv = xbuf[s, P0:P0 + C, :] * cw_ref[0:1, :]
        for j in range(1, CONV_W):
            conv = conv + xbuf[s, P0 + j:P0 + j + C, :] * cw_ref[j:j + 1, :]
        tail = xbuf[s, SUBLANES + C - (CONV_W - 1):SUBLANES + C, :]
        xbuf[s, P0:SUBLANES, :] = tail
        cn_ref[s] = tail
        qkv = _silu(conv)

        small = x_ref[rows, W + GDN_WIDTH:]
        beta_all = jax.nn.sigmoid(small)
        g_all = -jnp.exp(alog_ref[...]) * _softplus(small + dt_ref[...])
        b_all = _dot(tri, g_all, HI)

        for h in range(GDN_HEADS):
            qh = qkv[:, h * GDN_DK:(h + 1) * GDN_DK]
            kh = qkv[:, GDN_WIDTH + h * GDN_DK:GDN_WIDTH + (h + 1) * GDN_DK]
            vh = qkv[:, 2 * GDN_WIDTH + h * GDN_DV:2 * GDN_WIDTH + (h + 1) * GDN_DV]
            qh = qh * lax.rsqrt(jnp.sum(qh * qh, axis=-1, keepdims=True) + EPS) * (GDN_DK ** -0.5)
            kh = kh * lax.rsqrt(jnp.sum(kh * kh, axis=-1, keepdims=True) + EPS)
            beta = beta_all[:, h:h + 1]
            gcol = g_all[:, GDN_HEADS + h:GDN_HEADS + h + 1]
            bcol = b_all[:, GDN_HEADS + h:GDN_HEADS + h + 1]
            brow = _dot(ones, jnp.where(ri <= ci, gcol, 0.0), HI)
            decay = jnp.where(incl, jnp.exp(jnp.where(incl, bcol - brow, 0.0)), 0.0)
            kb = kh * beta
            nm = -jnp.where(strict, _dot_nt(kb, kh, HI) * decay, 0.0)
            tinv = eye + nm
            pw = nm
            for _ in range(n_sq):
                pw = _dot(pw, pw, HI)
                tinv = tinv + _dot(tinv, pw, HI)
            ebc = jnp.exp(bcol)
            u = _dot(tinv, vh * beta, HI)
            w = _dot(tinv, kb * ebc, HI)
            S = s_ref[s, h]
            v_new = u - _dot(w, S)
            qk = _dot_nt(qh, kh) * decay
            o = _dot(qh * ebc, S) + _dot(qk, v_new)
            b_last = bcol[C - 1:C]
            s_ref[s, h] = jnp.exp(b_last) * S + _dot_tn(kh * jnp.exp(b_last - bcol), v_new)
            zg = x_ref[rows, W + h * GDN_DV:W + (h + 1) * GDN_DV]
            o_ref[rows, h * GDN_DV:(h + 1) * GDN_DV] = _rms(o, gn_ref[...]) * _silu(zg)


def _gdn(x, s0, cbuf, cw, alog, dt, gn, *, B, L, row0, bb):
    C = min(CHUNK, L)
    nc = L // C
    rb = bb * C
    blk0 = row0 // rb
    W = 3 * GDN_WIDTH
    kern = functools.partial(_gdn_kernel, C=C, bb=bb)
    return pl.pallas_call(
        kern,
        grid=(B // bb, nc),
        in_specs=[pl.BlockSpec((rb, GDN_IN), lambda b, c: (blk0 + b * nc + c, 0)),
                  pl.BlockSpec((bb, GDN_HEADS, GDN_DK, GDN_DV), lambda b, c: (b, 0, 0, 0)),
                  pl.BlockSpec((bb, CONV_W - 1, W), lambda b, c: (b, 0, 0)),
                  pl.BlockSpec((CONV_W, W), lambda b, c: (0, 0)),
                  pl.BlockSpec((1, LANES), lambda b, c: (0, 0)),
                  pl.BlockSpec((1, LANES), lambda b, c: (0, 0)),
                  pl.BlockSpec((1, GDN_DV), lambda b, c: (0, 0))],
        out_specs=[pl.BlockSpec((rb, GDN_WIDTH), lambda b, c: (b * nc + c, 0)),
                   pl.BlockSpec((bb, GDN_HEADS, GDN_DK, GDN_DV), lambda b, c: (b, 0, 0, 0)),
                   pl.BlockSpec((bb, CONV_W - 1, W), lambda b, c: (b, 0, 0))],
        out_shape=[jax.ShapeDtypeStruct((B * L, GDN_WIDTH), F32),
                   jax.ShapeDtypeStruct((B, GDN_HEADS, GDN_DK, GDN_DV), F32),
                   jax.ShapeDtypeStruct((B, CONV_W - 1, W), F32)],
        scratch_shapes=[pltpu.VMEM((bb, SUBLANES + C, W), F32)],
        compiler_params=_cparams(("parallel", "arbitrary")),
        name="gdn",
    )(x, s0, cbuf, cw, alog, dt, gn)


def _mid_kernel(h_ref, o1_ref, o2_ref, wo_ref, g_ref, wqh_ref, wql_ref, keys_ref, h1_ref, xn_ref, st_ref):
    h1 = (h_ref[...] + _dot(o1_ref[...].astype(BF16), wo_ref[:GLA_WIDTH, :])
          + _dot(o2_ref[...].astype(BF16), wo_ref[GLA_WIDTH:, :]))
    h1_ref[...] = h1
    xn = _rms(h1, g_ref[...])
    x_hi = xn.astype(BF16)
    xn_ref[...] = x_hi
    x_lo = (xn - x_hi.astype(F32)).astype(BF16)
    q = _dot(x_hi, wqh_ref[...]) + (_dot(x_hi, wql_ref[...]) + _dot(x_lo, wqh_ref[...]))
    for hc in range(2 * PEER_HEADS):
        st_ref[hc] = _dot_nt(keys_ref[hc], q[:, hc * PEER_HALF:(hc + 1) * PEER_HALF], HI)


def _mid(h, o1, o2, wo, g, wqh, wql, keys):
    n = h.shape[0]
    tm = _tile(n, 256)
    nq = 2 * PEER_HEADS * PEER_HALF
    return pl.pallas_call(
        _mid_kernel,
        grid=(n // tm,),
        in_specs=[pl.BlockSpec((tm, D_MODEL), lambda i: (i, 0)),
                  pl.BlockSpec((tm, GLA_WIDTH), lambda i: (i, 0)),
                  pl.BlockSpec((tm, GDN_WIDTH), lambda i: (i, 0)),
                  pl.BlockSpec((D_MODEL, D_MODEL), lambda i: (0, 0)),
                  pl.BlockSpec((1, D_MODEL), lambda i: (0, 0)),
                  pl.BlockSpec((D_MODEL, nq), lambda i: (0, 0)),
                  pl.BlockSpec((D_MODEL, nq), lambda i: (0, 0)),
                  pl.BlockSpec((2 * PEER_HEADS, N_KEYS, PEER_HALF), lambda i: (0, 0, 0))],
        out_specs=[pl.BlockSpec((tm, D_MODEL), lambda i: (i, 0)),
                   pl.BlockSpec((tm, D_MODEL), lambda i: (i, 0)),
                   pl.BlockSpec((2 * PEER_HEADS, N_KEYS, tm), lambda i: (0, 0, i))],
        out_shape=[jax.ShapeDtypeStruct((n, D_MODEL), F32),
                   jax.ShapeDtypeStruct((n, D_MODEL), BF16),
                   jax.ShapeDtypeStruct((2 * PEER_HEADS, N_KEYS, n), F32)],
        compiler_params=_cparams(("parallel",)),
        name="mid",
    )(h, o1, o2, wo, g, wqh, wql, keys)


CAND_ROWS = 80


def _top16(S, io, n):
    vals = []
    for k in range(PEER_TOPK):
        m = jnp.max(S, axis=0, keepdims=True)
        vals.append(m)
        if k + 1 < PEER_TOPK:
            idx = jnp.min(jnp.where(S == m, io, n), axis=0, keepdims=True)
            S = jnp.where(io == idx, -jnp.inf, S)
    return vals


def _topk_kernel(s_ref, e_ref, tau_ref, v_ref, c_ref, *, tm):
    io = lax.broadcasted_iota(jnp.int32, (N_KEYS, tm), 0)
    io_c = lax.broadcasted_iota(jnp.int32, (CAND_ROWS, tm), 0)
    io8 = lax.broadcasted_iota(jnp.int32, (SUBLANES, tm), 0)

    def head(h, carry):
        for c in range(2):
            vals = _top16(s_ref[2 * h + c], io, N_KEYS)
            for k in range(PEER_TOPK):
                v_ref[c, k:k + 1, :] = vals[k]
        c_ref[0:16, :] = v_ref[0, 0:1, :] + v_ref[1, 0:16, :]
        off = 16
        for i in range(1, 8):
            nj = PEER_TOPK // (i + 1)
            c_ref[off:off + SUBLANES, :] = jnp.where(io8 < nj, v_ref[0, i:i + 1, :] + v_ref[1, 0:SUBLANES, :], -jnp.inf)
            off += SUBLANES
        c_ref[off:off + SUBLANES, :] = v_ref[0, 8:16, :] + v_ref[1, 0:1, :]
        cv = _top16(c_ref[...], io_c, CAND_ROWS)
        z = jnp.ones_like(cv[0])
        for k in range(1, PEER_TOPK):
            z = z + jnp.exp(cv[k] - cv[0])
        tau_ref[pl.ds(h, 1), :] = cv[PEER_TOPK - 1]
        s1 = s_ref[2 * h]
        s2 = s_ref[2 * h + 1]
        e_ref[2 * h] = jnp.exp(s1 - v_ref[0, 0:1, :]) * (1.0 / z)
        e_ref[2 * h + 1] = jnp.exp(s2 - v_ref[1, 0:1, :])
        return carry

    lax.fori_loop(0, PEER_HEADS, head, 0)


def _topk(st):
    n = st.shape[-1]
    tm = _tile(n, 128)
    kern = functools.partial(_topk_kernel, tm=tm)
    return pl.pallas_call(
        kern,
        grid=(n // tm,),
        in_specs=[pl.BlockSpec((2 * PEER_HEADS, N_KEYS, tm), lambda i: (0, 0, i))],
        out_specs=[pl.BlockSpec((2 * PEER_HEADS, N_KEYS, tm), lambda i: (0, 0, i)),
                   pl.BlockSpec((PEER_HEADS, tm), lambda i: (0, i))],
        out_shape=[jax.ShapeDtypeStruct((2 * PEER_HEADS, N_KEYS, n), F32),
                   jax.ShapeDtypeStruct((PEER_HEADS, n), F32)],
        scratch_shapes=[pltpu.VMEM((2, PEER_TOPK, tm), F32), pltpu.VMEM((CAND_ROWS, tm), F32)],
        compiler_params=_cparams(("parallel",)),
        name="topk",
    )(st)


def _peer_kernel(xn_ref, s_ref, e_ref, tau_ref, u_ref, vt_ref, out_ref, acc_ref, st_ref, p_ref, *, te, tm):
    j = pl.program_id(1)
    n1 = te // N_KEYS

    @pl.when(j == 0)
    def _():
        acc_ref[...] = jnp.zeros_like(acc_ref)

    st_ref[...] = _dot_nt(u_ref[...], xn_ref[...])

    def body(jj, carry):
        i1 = j * n1 + jj
        g = jnp.zeros((N_KEYS, tm), F32)
        for h in range(PEER_HEADS):
            t = s_ref[2 * h, pl.ds(i1, 1), :] + s_ref[2 * h + 1]
            w = e_ref[2 * h, pl.ds(i1, 1), :] * e_ref[2 * h + 1]
            g = g + jnp.where(t >= tau_ref[h:h + 1, :], w, 0.0)
        r = pl.multiple_of(jj * N_KEYS, N_KEYS)
        hid = st_ref[pl.ds(r, N_KEYS), :]
        act = 0.5 * hid * (1.0 + lax.erf(hid * (0.5 ** 0.5)))
        p_ref[pl.ds(r, N_KEYS), :] = (act * g).astype(BF16)
        return carry

    lax.fori_loop(0, n1, body, 0)
    acc_ref[...] += _dot(vt_ref[...], p_ref[...])

    @pl.when(j == pl.num_programs(1) - 1)
    def _():
        out_ref[...] = acc_ref[...].T


def _peer(xn, st, e, tau, u, vt, te=1024):
    n = xn.shape[0]
    tm = _tile(n, 512)
    kern = functools.partial(_peer_kernel, te=te, tm=tm)
    return pl.pallas_call(
        kern,
        grid=(n // tm, N_EXPERTS // te),
        in_specs=[pl.BlockSpec((tm, D_MODEL), lambda i, j: (i, 0)),
                  pl.BlockSpec((2 * PEER_HEADS, N_KEYS, tm), lambda i, j: (0, 0, i)),
                  pl.BlockSpec((2 * PEER_HEADS, N_KEYS, tm), lambda i, j: (0, 0, i)),
                  pl.BlockSpec((PEER_HEADS, tm), lambda i, j: (0, i)),
                  pl.BlockSpec((te, D_MODEL), lambda i, j: (j, 0)),
                  pl.BlockSpec((D_MODEL, te), lambda i, j: (0, j))],
        out_specs=pl.BlockSpec((tm, D_MODEL), lambda i, j: (i, 0)),
        out_shape=jax.ShapeDtypeStruct((n, D_MODEL), F32),
        scratch_shapes=[pltpu.VMEM((D_MODEL, tm), F32), pltpu.VMEM((te, tm), F32), pltpu.VMEM((te, tm), BF16)],
        compiler_params=_cparams(("parallel", "arbitrary")),
        name="peer",
    )(xn, st, e, tau, u, vt)


def _ple_kernel(h1_ref, po_ref, p_ref, g_ref, wg_ref, bg_ref, wp_ref, gf_ref, out_ref, *, final):
    h2 = h1_ref[...] + po_ref[...]
    gate = jax.nn.sigmoid(_dot(_rms(h2, g_ref[...]).astype(BF16), wg_ref[...]) + bg_ref[...])
    h3 = h2 + gate * _dot(p_ref[...].astype(BF16), wp_ref[...])
    out_ref[...] = _rms(h3, gf_ref[...]) if final else h3


def _ple(h1, po, p, g, wg, bg, wp, gf, final):
    n = h1.shape[0]
    tm = _tile(n, 512)
    kern = functools.partial(_ple_kernel, final=final)
    row = lambda i: (i, 0)
    fix = lambda i: (0, 0)
    return pl.pallas_call(
        kern,
        grid=(n // tm,),
        in_specs=[pl.BlockSpec((tm, D_MODEL), row), pl.BlockSpec((tm, D_MODEL), row),
                  pl.BlockSpec((tm, PLE_DIM), row), pl.BlockSpec((1, D_MODEL), fix),
                  pl.BlockSpec((D_MODEL, D_MODEL), fix), pl.BlockSpec((1, D_MODEL), fix),
                  pl.BlockSpec((PLE_DIM, D_MODEL), fix), pl.BlockSpec((1, D_MODEL), fix)],
        out_specs=pl.BlockSpec((tm, D_MODEL), row),
        out_shape=jax.ShapeDtypeStruct((n, D_MODEL), F32),
        compiler_params=_cparams(("parallel",)),
        name="ple",
    )(h1, po, p, g, wg, bg, wp, gf)


def _pad_lanes(x, width=LANES):
    pad = [(0, 0)] * (x.ndim - 1) + [(0, width - x.shape[-1])]
    return jnp.pad(x, pad)


def _trunk(h, p_all, groups, states, prm):
    (norm_mix, w_in_p, wup_p, b_gla_gate, gla_norm, conv_w, alog_row, dt_row, gdn_norm, w_out_b,
     norm_ffn, wq_hi, wq_lo, keys, u_b, vt_b, norm_ple, w_ple_gate_b, b_ple_gate, w_ple_proj_b, norm_final) = prm
    new_states = [([], [], []) for _ in groups]
    depth = w_in_p.shape[0]
    for i in range(depth):
        gla_in, gdn_in = _in_proj(h, norm_mix[i][None], w_in_p[i])
        o1s, o2s = [], []
        for gi, (B, L, row0, bb) in enumerate(groups):
            s_gla, s_gdn, s_conv = states[gi]
            o1, sg = _gla(gla_in, s_gla[i].reshape(B, 2, LANES, GLA_DV), wup_p[i], b_gla_gate[i][None],
                          gla_norm[i][None], B=B, L=L, row0=row0, bb=bb)
            o2, sd, sc = _gdn(gdn_in, s_gdn[i], s_conv[i], conv_w[i], alog_row[i], dt_row[i], gdn_norm[i][None],
                              B=B, L=L, row0=row0, bb=bb)
            o1s.append(o1)
            o2s.append(o2)
            new_states[gi][0].append(sg.reshape(B, GLA_HEADS, GLA_DK, GLA_DV))
            new_states[gi][1].append(sd)
            new_states[gi][2].append(sc)
        o1 = jnp.concatenate(o1s, axis=0)
        o2 = jnp.concatenate(o2s, axis=0)
        h1, xn, st = _mid(h, o1, o2, w_out_b[i], norm_ffn[i][None], wq_hi[i], wq_lo[i], keys[i])
        e, tau = _topk(st)
        po = _peer(xn, st, e, tau, u_b[i], vt_b[i])
        h = _ple(h1, po, p_all[i], norm_ple[i][None], w_ple_gate_b[i], b_ple_gate[i][None], w_ple_proj_b[i],
                 norm_final[None], final=(i == depth - 1))
    return h, [tuple(jnp.stack(x) for x in ns) for ns in new_states]


def kernel(x_prompt, x_sample, state_gla, state_gdn, state_conv, p_prompt, p_sample, norm_mix, w_in, w_gla_up, b_gla_gate, gla_norm, conv_w, gdn_a_log, gdn_dt_bias, gdn_norm, w_out, norm_ffn, w_peer_q, peer_sub_keys, peer_u, peer_v, norm_ple, w_ple_gate, b_ple_gate, w_ple_proj, norm_final):
    Bp, Lp, _ = x_prompt.shape
    Bs, Ls, _ = x_sample.shape
    n_p, n_s = Bp * Lp, Bs * Ls
    depth = w_in.shape[0]

    c0 = 2 * GLA_KW + 2 * GLA_WIDTH
    c1 = c0 + GLA_RANK
    c2 = c1 + 3 * GDN_WIDTH + GDN_WIDTH
    w_in_p = jnp.concatenate([w_in[..., :c0], _pad_lanes(w_in[..., c0:c1]), w_in[..., c1:c2],
                              _pad_lanes(w_in[..., c2:])], axis=-1).astype(BF16)
    wup_p = jnp.pad(w_gla_up, ((0, 0), (0, LANES - GLA_RANK), (0, 0)))
    lane_pad = ((0, 0), (GDN_HEADS, LANES - 2 * GDN_HEADS))
    alog_row = jnp.pad(gdn_a_log, lane_pad)[:, None, :]
    dt_row = jnp.pad(gdn_dt_bias, lane_pad)[:, None, :]
    wq_hi = w_peer_q.astype(BF16)
    wq_lo = (w_peer_q - wq_hi.astype(F32)).astype(BF16)
    keys = peer_sub_keys.reshape(depth, 2 * PEER_HEADS, N_KEYS, PEER_HALF)
    u_b = peer_u.astype(BF16)
    vt_b = jnp.swapaxes(peer_v, 1, 2).astype(BF16)
    prm = (norm_mix, w_in_p, wup_p, b_gla_gate, gla_norm, conv_w, alog_row, dt_row, gdn_norm, w_out.astype(BF16),
           norm_ffn, wq_hi, wq_lo, keys, u_b, vt_b, norm_ple, w_ple_gate.astype(BF16), b_ple_gate,
           w_ple_proj.astype(BF16), norm_final)

    h = jnp.concatenate([x_prompt.reshape(n_p, D_MODEL), x_sample.reshape(n_s, D_MODEL)], axis=0)
    p_all = jnp.concatenate([p_prompt.reshape(depth, n_p, PLE_DIM), p_sample.reshape(depth, n_s, PLE_DIM)], axis=1)
    zeros = lambda *s: jnp.zeros(s, F32)
    st_p = (zeros(depth, Bp, GLA_HEADS, GLA_DK, GLA_DV), zeros(depth, Bp, GDN_HEADS, GDN_DK, GDN_DV),
            zeros(depth, Bp, CONV_W - 1, 3 * GDN_WIDTH))
    st_s = (state_gla, state_gdn, state_conv)
    bb_s = max(1, CHUNK // Ls)
    groups = [(Bp, Lp, 0, 1), (Bs, Ls, n_p, bb_s)]
    y, (ns_p, ns_s) = _trunk(h, p_all, groups, [st_p, st_s], prm)
    y_prompt = y[:n_p].reshape(Bp, Lp, D_MODEL)
    y_sample = y[n_p:].reshape(Bs, Ls, D_MODEL)
    return (y_prompt, y_sample, ns_p[0], ns_p[1], ns_p[2], ns_s[0], ns_s[1], ns_s[2])
```

```python
import functools

import jax
import jax.numpy as jnp
from jax import lax
from jax.experimental import pallas as pl
from jax.experimental.pallas import tpu as pltpu

F32 = jnp.float32
BF16 = jnp.bfloat16
HI = lax.Precision.HIGHEST

D_MODEL = 1024
DEPTH = 4
GLA_HEADS = 4
GLA_WIDTH = 512
GLA_DV = 128
GLA_DK = 64
GLA_KW = 256
GLA_RANK = 16
GLA_GATE_NORM = 16.0
GDN_HEADS = 4
GDN_WIDTH = 512
GDN_DK = 128
GDN_DV = 128
CONV_W = 4
CHUNK = 64
PEER_HEADS = 8
PEER_HALF = 128
N_KEYS = 128
N_EXPERTS = N_KEYS * N_KEYS
PEER_TOPK = 16
PLE_DIM = 256
EPS = 1e-6

LANES = 128
SUBLANES = 8
GLA_IN = 2 * GLA_KW + 2 * GLA_WIDTH + LANES
GDN_IN = 3 * GDN_WIDTH + GDN_WIDTH + LANES
GLA_SUB = 16
SAMPLE_SEQS_PER_STEP = 16
VMEM_LIMIT = 56 * 1024 * 1024


def _tile(n, pref):
    t = pref
    while n % t:
        t -= LANES
    return t


def _cparams(sem):
    return pltpu.CompilerParams(dimension_semantics=sem, vmem_limit_bytes=VMEM_LIMIT)


def _dot(a, b, prec=None):
    return jnp.dot(a, b, precision=prec, preferred_element_type=F32)


def _dot_nt(a, b, prec=None):
    return lax.dot_general(a, b, (((1,), (1,)), ((), ())), precision=prec, preferred_element_type=F32)


def _bmm(a, b, prec=None):
    return lax.dot_general(a, b, (((2,), (1,)), ((0,), (0,))), precision=prec, preferred_element_type=F32)


def _bmm_nt(a, b, prec=None):
    return lax.dot_general(a, b, (((2,), (2,)), ((0,), (0,))), precision=prec, preferred_element_type=F32)


def _bmm_tn(a, b, prec=None):
    return lax.dot_general(a, b, (((1,), (1,)), ((0,), (0,))), precision=prec, preferred_element_type=F32)


def _split(a):
    hi = a.astype(BF16)
    return hi, (a - hi.astype(F32)).astype(BF16)


def _bmm3(a, b, nt=False):
    f = _bmm_nt if nt else _bmm
    return f(a[0], b[0]) + (f(a[0], b[1]) + f(a[1], b[0]))


def _rms(x, g):
    return x * lax.rsqrt(jnp.mean(x * x, axis=-1, keepdims=True) + EPS) * g


def _softplus(x):
    return jnp.maximum(x, 0.0) + jnp.log1p(jnp.exp(-jnp.abs(x)))


def _silu(x):
    return x * jax.nn.sigmoid(x)


def _in_proj_kernel(h_ref, g_ref, w_ref, gla_ref, gdn_ref):
    xn = _rms(h_ref[...], g_ref[...]).astype(BF16)
    gla_ref[...] = _dot(xn, w_ref[:, :GLA_IN])
    gdn_ref[...] = _dot(xn, w_ref[:, GLA_IN:])


def _in_proj(h, g, w):
    n = h.shape[0]
    tm = _tile(n, 256)
    return pl.pallas_call(
        _in_proj_kernel,
        grid=(n // tm,),
        in_specs=[pl.BlockSpec((tm, D_MODEL), lambda i: (i, 0)),
                  pl.BlockSpec((1, D_MODEL), lambda i: (0, 0)),
                  pl.BlockSpec((D_MODEL, GLA_IN + GDN_IN), lambda i: (0, 0))],
        out_specs=[pl.BlockSpec((tm, GLA_IN), lambda i: (i, 0)),
                   pl.BlockSpec((tm, GDN_IN), lambda i: (i, 0))],
        out_shape=[jax.ShapeDtypeStruct((n, GLA_IN), F32), jax.ShapeDtypeStruct((n, GDN_IN), F32)],
        compiler_params=_cparams(("parallel",)),
        name="in_proj",
    )(h, g, w)


def _gla_kernel(x_ref, s0_ref, wup_ref, bg_ref, gn_ref, o_ref, s_ref, *, C, SB, bb):
    c = pl.program_id(1)

    @pl.when(c == 0)
    def _():
        s_ref[...] = s0_ref[...]

    nb = C // SB
    G = 2 * bb
    lane = lax.broadcasted_iota(jnp.int32, (1, 1, LANES), 2)
    hmask = (lane < GLA_DK, lane >= GLA_DK)
    tri = (lax.broadcasted_iota(jnp.int32, (bb, C, C), 1) >= lax.broadcasted_iota(jnp.int32, (bb, C, C), 2)).astype(F32)
    t_sb = lax.broadcasted_iota(jnp.int32, (1, SB, 1), 1)
    col_c = lax.broadcasted_iota(jnp.int32, (1, SB, C), 2)
    eye = lax.broadcasted_iota(jnp.int32, (1, LANES, LANES), 1) == lax.broadcasted_iota(jnp.int32, (1, LANES, LANES), 2)
    row_l = lax.broadcasted_iota(jnp.int32, (1, LANES, 1), 1)
    c0 = 2 * GLA_KW + 2 * GLA_WIDTH

    def pairs(a, off, width):
        parts = [a[:, off + p * width:off + (p + 1) * width].reshape(bb, 1, C, width) for p in range(2)]
        return jnp.concatenate(parts, axis=1).reshape(G, C, width)

    z = _dot(x_ref[:, c0:], wup_ref[...], HI) + bg_ref[...]
    gk = (jnp.minimum(z, 0.0) - jnp.log1p(jnp.exp(-jnp.abs(z)))) * (1.0 / GLA_GATE_NORM)
    b_all = _bmm(tri, gk.reshape(bb, C, GLA_KW), HI).reshape(bb * C, GLA_KW)
    q = pairs(x_ref[:, :GLA_KW], 0, LANES) * (GLA_DK ** -0.5)
    k = pairs(x_ref[:, GLA_KW:2 * GLA_KW], 0, LANES)
    b = pairs(b_all, 0, LANES)
    v = pairs(x_ref[:, 2 * GLA_KW:2 * GLA_KW + GLA_WIDTH], 0, 2 * GLA_DV)
    r = pairs(x_ref[:, 2 * GLA_KW + GLA_WIDTH:c0], 0, 2 * GLA_DV)
    S = s_ref[...].reshape(G, LANES, GLA_DV)

    o_blocks = ([], [])
    for i in range(nb):
        bi, qi, ki = b[:, i * SB:(i + 1) * SB], q[:, i * SB:(i + 1) * SB], k[:, i * SB:(i + 1) * SB]
        att_d = [jnp.zeros((G, SB, C), F32), jnp.zeros((G, SB, C), F32)]
        for j in range(SB):
            e = jnp.exp(jnp.where(t_sb >= j, bi - bi[:, j:j + 1], 0.0))
            term = qi * ki[:, j:j + 1] * e
            for h in range(2):
                col = jnp.sum(jnp.where(hmask[h], term, 0.0), axis=-1, keepdims=True)
                att_d[h] = jnp.where(col_c == i * SB + j, jnp.where(t_sb >= j, col, 0.0), att_d[h])
        if i > 0:
            ri = b[:, i * SB - 1:i * SB]
            qd = qi * jnp.exp(bi - ri)
            kd = k * jnp.exp(jnp.minimum(ri - b, 0.0))
        for h in range(2):
            att = att_d[h]
            if i > 0:
                att = jnp.where(col_c < i * SB, _bmm_nt(jnp.where(hmask[h], qd, 0.0), kd), att)
            o_blocks[h].append(_bmm(att, v[:, :, h * GLA_DV:(h + 1) * GLA_DV]))

    qe = q * jnp.exp(b)
    b_last = b[:, C - 1:C]
    upd = _bmm_tn(k * jnp.exp(b_last - b), v)
    dec = jnp.where(eye, jnp.broadcast_to(jnp.exp(b_last), (G, LANES, LANES)), 0.0)
    s_new = _bmm(dec, S, HI) + jnp.where(row_l < GLA_DK, upd[:, :, :GLA_DV], upd[:, :, GLA_DV:])
    s_ref[...] = s_new.reshape(bb, 2, LANES, GLA_DV)
    for h in range(2):
        o_intra = o_blocks[h][0] if nb == 1 else jnp.concatenate(o_blocks[h], axis=1)
        o = o_intra + _bmm(jnp.where(hmask[h], qe, 0.0), S)
        y = (_rms(o, gn_ref[...]) * _silu(r[:, :, h * GLA_DV:(h + 1) * GLA_DV])).reshape(bb, 2, C, GLA_DV)
        for p in range(2):
            hh = 2 * p + h
            o_ref[:, hh * GLA_DV:(hh + 1) * GLA_DV] = y[:, p].reshape(bb * C, GLA_DV)


def _gla(x, s0, wup, bg, gn, *, B, L, row0, bb):
    C = min(CHUNK, L)
    SB = min(GLA_SUB, C)
    nc = L // C
    rb = bb * C
    blk0 = row0 // rb
    kern = functools.partial(_gla_kernel, C=C, SB=SB, bb=bb)
    return pl.pallas_call(
        kern,
        grid=(B // bb, nc),
        in_specs=[pl.BlockSpec((rb, GLA_IN), lambda b, c: (blk0 + b * nc + c, 0)),
                  pl.BlockSpec((bb, 2, LANES, GLA_DV), lambda b, c: (b, 0, 0, 0)),
                  pl.BlockSpec((LANES, GLA_KW), lambda b, c: (0, 0)),
                  pl.BlockSpec((1, GLA_KW), lambda b, c: (0, 0)),
                  pl.BlockSpec((1, GLA_DV), lambda b, c: (0, 0))],
        out_specs=[pl.BlockSpec((rb, GLA_WIDTH), lambda b, c: (b * nc + c, 0)),
                   pl.BlockSpec((bb, 2, LANES, GLA_DV), lambda b, c: (b, 0, 0, 0))],
        out_shape=[jax.ShapeDtypeStruct((B * L, GLA_WIDTH), F32),
                   jax.ShapeDtypeStruct((B, 2, LANES, GLA_DV), F32)],
        compiler_params=_cparams(("parallel", "arbitrary")),
        name="gla",
    )(x, s0, wup, bg, gn)


def _gdn_kernel(x_ref, s0_ref, cb_ref, cw_ref, alog_ref, dt_ref, gn_ref, o_ref, s_ref, cn_ref, xbuf, *, C, bb):
    c = pl.program_id(1)
    W = 3 * GDN_WIDTH
    P0 = SUBLANES - (CONV_W - 1)
    G = bb * GDN_HEADS

    @pl.when(c == 0)
    def _():
        s_ref[...] = s0_ref[...]
        xbuf[:, P0:SUBLANES, :] = cb_ref[...]

    ri = lax.broadcasted_iota(jnp.int32, (1, C, C), 1)
    ci = lax.broadcasted_iota(jnp.int32, (1, C, C), 2)
    incl = ri >= ci
    strict = ri > ci
    eye = (ri == ci).astype(F32)
    tri = jnp.broadcast_to(incl.astype(F32), (bb, C, C))
    ones = jnp.ones((G, C, C), F32)
    n_sq = max(C.bit_length() - 2, 0)

    def heads(a, off, width):
        parts = [a[:, :, off + h * width:off + (h + 1) * width].reshape(bb, 1, C, width) for h in range(GDN_HEADS)]
        return jnp.concatenate(parts, axis=1).reshape(G, C, width)

    xbuf[:, SUBLANES:SUBLANES + C, :] = x_ref[:, :W].reshape(bb, C, W)
    conv = xbuf[:, P0:P0 + C, :] * cw_ref[0:1, :]
    for j in range(1, CONV_W):
        conv = conv + xbuf[:, P0 + j:P0 + j + C, :] * cw_ref[j:j + 1, :]
    tail = xbuf[:, SUBLANES + C - (CONV_W - 1):SUBLANES + C, :]
    xbuf[:, P0:SUBLANES, :] = tail
    cn_ref[...] = tail
    qkv = _silu(conv)

    small = x_ref[:, W + GDN_WIDTH:].reshape(bb, C, LANES)
    beta_all = jax.nn.sigmoid(small)
    g_all = -jnp.exp(alog_ref[...]) * _softplus(small + dt_ref[...])
    b_all = _bmm(tri, g_all, HI)

    qh = heads(qkv, 0, GDN_DK)
    kh = heads(qkv, GDN_WIDTH, GDN_DK)
    vh = heads(qkv, 2 * GDN_WIDTH, GDN_DV)
    qh = qh * lax.rsqrt(jnp.sum(qh * qh, axis=-1, keepdims=True) + EPS) * (GDN_DK ** -0.5)
    kh = kh * lax.rsqrt(jnp.sum(kh * kh, axis=-1, keepdims=True) + EPS)
    beta = heads(beta_all, 0, 1)
    gcol = heads(g_all, GDN_HEADS, 1)
    bcol = heads(b_all, GDN_HEADS, 1)
    brow = _bmm(ones, jnp.where(ri <= ci, gcol, 0.0), HI)
    decay = jnp.where(incl, jnp.exp(jnp.where(incl, bcol - brow, 0.0)), 0.0)
    kb = kh * beta
    nm = -jnp.where(strict, _bmm3(_split(kb), _split(kh), nt=True) * decay, 0.0)
    tinv = eye + nm
    pw = nm
    for _ in range(n_sq):
        pws = _split(pw)
        pw = _bmm3(pws, pws)
        tinv = tinv + _bmm3(_split(tinv), _split(pw))
    ebc = jnp.exp(bcol)
    tis = _split(tinv)
    u = _bmm3(tis, _split(vh * beta))
    w = _bmm3(tis, _split(kb * ebc))
    S = s_ref[...].reshape(G, GDN_DK, GDN_DV)
    v_new = u - _bmm(w, S)
    qk = _bmm_nt(qh, kh) * decay
    o = _bmm(qh * ebc, S) + _bmm(qk, v_new)
    b_last = bcol[:, C - 1:C]
    s_new = jnp.exp(b_last) * S + _bmm_tn(kh * jnp.exp(b_last - bcol), v_new)
    s_ref[...] = s_new.reshape(bb, GDN_HEADS, GDN_DK, GDN_DV)
    zg = heads(x_ref[:, W:W + GDN_WIDTH].reshape(bb, C, GDN_WIDTH), 0, GDN_DV)
    y = (_rms(o, gn_ref[...]) * _silu(zg)).reshape(bb, GDN_HEADS, C, GDN_DV)
    for h in range(GDN_HEADS):
        o_ref[:, h * GDN_DV:(h + 1) * GDN_DV] = y[:, h].reshape(bb * C, GDN_DV)


def _gdn(x, s0, cbuf, cw, alog, dt, gn, *, B, L, row0, bb):
    C = min(CHUNK, L)
    nc = L // C
    rb = bb * C
    blk0 = row0 // rb
    W = 3 * GDN_WIDTH
    kern = functools.partial(_gdn_kernel, C=C, bb=bb)
    return pl.pallas_call(
        kern,
        grid=(B // bb, nc),
        in_specs=[pl.BlockSpec((rb, GDN_IN), lambda b, c: (blk0 + b * nc + c, 0)),
                  pl.BlockSpec((bb, GDN_HEADS, GDN_DK, GDN_DV), lambda b, c: (b, 0, 0, 0)),
                  pl.BlockSpec((bb, CONV_W - 1, W), lambda b, c: (b, 0, 0)),
                  pl.BlockSpec((CONV_W, W), lambda b, c: (0, 0)),
                  pl.BlockSpec((1, LANES), lambda b, c: (0, 0)),
                  pl.BlockSpec((1, LANES), lambda b, c: (0, 0)),
                  pl.BlockSpec((1, GDN_DV), lambda b, c: (0, 0))],
        out_specs=[pl.BlockSpec((rb, GDN_WIDTH), lambda b, c: (b * nc + c, 0)),
                   pl.BlockSpec((bb, GDN_HEADS, GDN_DK, GDN_DV), lambda b, c: (b, 0, 0, 0)),
                   pl.BlockSpec((bb, CONV_W - 1, W), lambda b, c: (b, 0, 0))],
        out_shape=[jax.ShapeDtypeStruct((B * L, GDN_WIDTH), F32),
                   jax.ShapeDtypeStruct((B, GDN_HEADS, GDN_DK, GDN_DV), F32),
                   jax.ShapeDtypeStruct((B, CONV_W - 1, W), F32)],
        scratch_shapes=[pltpu.VMEM((bb, SUBLANES + C, W), F32)],
        compiler_params=_cparams(("parallel", "arbitrary")),
        name="gdn",
    )(x, s0, cbuf, cw, alog, dt, gn)


def _mid_kernel(h_ref, o1_ref, o2_ref, wo_ref, g_ref, wqh_ref, wql_ref, keys_ref, h1_ref, xn_ref, st_ref):
    h1 = (h_ref[...] + _dot(o1_ref[...].astype(BF16), wo_ref[:GLA_WIDTH, :])
          + _dot(o2_ref[...].astype(BF16), wo_ref[GLA_WIDTH:, :]))
    h1_ref[...] = h1
    xn = _rms(h1, g_ref[...])
    x_hi = xn.astype(BF16)
    xn_ref[...] = x_hi
    x_lo = (xn - x_hi.astype(F32)).astype(BF16)
    q = _dot(x_hi, wqh_ref[...]) + (_dot(x_hi, wql_ref[...]) + _dot(x_lo, wqh_ref[...]))
    for hc in range(2 * PEER_HEADS):
        st_ref[hc] = _dot_nt(keys_ref[hc], q[:, hc * PEER_HALF:(hc + 1) * PEER_HALF], HI)


def _mid(h, o1, o2, wo, g, wqh, wql, keys):
    n = h.shape[0]
    tm = _tile(n, 256)
    nq = 2 * PEER_HEADS * PEER_HALF
    return pl.pallas_call(
        _mid_kernel,
        grid=(n // tm,),
        in_specs=[pl.BlockSpec((tm, D_MODEL), lambda i: (i, 0)),
                  pl.BlockSpec((tm, GLA_WIDTH), lambda i: (i, 0)),
                  pl.BlockSpec((tm, GDN_WIDTH), lambda i: (i, 0)),
                  pl.BlockSpec((D_MODEL, D_MODEL), lambda i: (0, 0)),
                  pl.BlockSpec((1, D_MODEL), lambda i: (0, 0)),
                  pl.BlockSpec((D_MODEL, nq), lambda i: (0, 0)),
                  pl.BlockSpec((D_MODEL, nq), lambda i: (0, 0)),
                  pl.BlockSpec((2 * PEER_HEADS, N_KEYS, PEER_HALF), lambda i: (0, 0, 0))],
        out_specs=[pl.BlockSpec((tm, D_MODEL), lambda i: (i, 0)),
                   pl.BlockSpec((tm, D_MODEL), lambda i: (i, 0)),
                   pl.BlockSpec((2 * PEER_HEADS, N_KEYS, tm), lambda i: (0, 0, i))],
        out_shape=[jax.ShapeDtypeStruct((n, D_MODEL), F32),
                   jax.ShapeDtypeStruct((n, D_MODEL), BF16),
                   jax.ShapeDtypeStruct((2 * PEER_HEADS, N_KEYS, n), F32)],
        compiler_params=_cparams(("parallel",)),
        name="mid",
    )(h, o1, o2, wo, g, wqh, wql, keys)


CAND_ROWS = 80


def _top16(S, io, n):
    vals = []
    for k in range(PEER_TOPK):
        m = jnp.max(S, axis=0, keepdims=True)
        vals.append(m)
        if k + 1 < PEER_TOPK:
            idx = jnp.min(jnp.where(S == m, io, n), axis=0, keepdims=True)
            S = jnp.where(io == idx, -jnp.inf, S)
    return vals


def _topk_kernel(s_ref, e_ref, tau_ref, v_ref, c_ref, *, tm):
    io = lax.broadcasted_iota(jnp.int32, (N_KEYS, tm), 0)
    io_c = lax.broadcasted_iota(jnp.int32, (CAND_ROWS, tm), 0)
    io8 = lax.broadcasted_iota(jnp.int32, (SUBLANES, tm), 0)

    def head(h, carry):
        for c in range(2):
            vals = _top16(s_ref[2 * h + c], io, N_KEYS)
            for k in range(PEER_TOPK):
                v_ref[c, k:k + 1, :] = vals[k]
        c_ref[0:16, :] = v_ref[0, 0:1, :] + v_ref[1, 0:16, :]
        off = 16
        for i in range(1, 8):
            nj = PEER_TOPK // (i + 1)
            c_ref[off:off + SUBLANES, :] = jnp.where(io8 < nj, v_ref[0, i:i + 1, :] + v_ref[1, 0:SUBLANES, :], -jnp.inf)
            off += SUBLANES
        c_ref[off:off + SUBLANES, :] = v_ref[0, 8:16, :] + v_ref[1, 0:1, :]
        cv = _top16(c_ref[...], io_c, CAND_ROWS)
        z = jnp.ones_like(cv[0])
        for k in range(1, PEER_TOPK):
            z = z + jnp.exp(cv[k] - cv[0])
        tau_ref[pl.ds(h, 1), :] = cv[PEER_TOPK - 1]
        s1 = s_ref[2 * h]
        s2 = s_ref[2 * h + 1]
        e_ref[2 * h] = jnp.exp(s1 - v_ref[0, 0:1, :]) * (1.0 / z)
        e_ref[2 * h + 1] = jnp.exp(s2 - v_ref[1, 0:1, :])
        return carry

    lax.fori_loop(0, PEER_HEADS, head, 0)


def _topk(st):
    n = st.shape[-1]
    tm = _tile(n, 128)
    kern = functools.partial(_topk_kernel, tm=tm)
    return pl.pallas_call(
        kern,
        grid=(n // tm,),
        in_specs=[pl.BlockSpec((2 * PEER_HEADS, N_KEYS, tm), lambda i: (0, 0, i))],
        out_specs=[pl.BlockSpec((2 * PEER_HEADS, N_KEYS, tm), lambda i: (0, 0, i)),
                   pl.BlockSpec((PEER_HEADS, tm), lambda i: (0, i))],
        out_shape=[jax.ShapeDtypeStruct((2 * PEER_HEADS, N_KEYS, n), F32),
                   jax.ShapeDtypeStruct((PEER_HEADS, n), F32)],
        scratch_shapes=[pltpu.VMEM((2, PEER_TOPK, tm), F32), pltpu.VMEM((CAND_ROWS, tm), F32)],
        compiler_params=_cparams(("parallel",)),
        name="topk",
    )(st)


def _peer_kernel(xn_ref, s_ref, e_ref, tau_ref, u_ref, vt_ref, out_ref, acc_ref, st_ref, p_ref, *, te, tm):
    j = pl.program_id(1)
    n1 = te // N_KEYS

    @pl.when(j == 0)
    def _():
        acc_ref[...] = jnp.zeros_like(acc_ref)

    st_ref[...] = _dot_nt(u_ref[...], xn_ref[...])

    def body(jj, carry):
        i1 = j * n1 + jj
        g = jnp.zeros((N_KEYS, tm), F32)
        for h in range(PEER_HEADS):
            t = s_ref[2 * h, pl.ds(i1, 1), :] + s_ref[2 * h + 1]
            w = e_ref[2 * h, pl.ds(i1, 1), :] * e_ref[2 * h + 1]
            g = g + jnp.where(t >= tau_ref[h:h + 1, :], w, 0.0)
        r = pl.multiple_of(jj * N_KEYS, N_KEYS)
        hid = st_ref[pl.ds(r, N_KEYS), :]
        act = 0.5 * hid * (1.0 + lax.erf(hid * (0.5 ** 0.5)))
        p_ref[pl.ds(r, N_KEYS), :] = (act * g).astype(BF16)
        return carry

    lax.fori_loop(0, n1, body, 0)
    acc_ref[...] += _dot(vt_ref[...], p_ref[...])

    @pl.when(j == pl.num_programs(1) - 1)
    def _():
        out_ref[...] = acc_ref[...].T


def _peer(xn, st, e, tau, u, vt, te=1024):
    n = xn.shape[0]
    tm = _tile(n, 512)
    kern = functools.partial(_peer_kernel, te=te, tm=tm)
    return pl.pallas_call(
        kern,
        grid=(n // tm, N_EXPERTS // te),
        in_specs=[pl.BlockSpec((tm, D_MODEL), lambda i, j: (i, 0)),
                  pl.BlockSpec((2 * PEER_HEADS, N_KEYS, tm), lambda i, j: (0, 0, i)),
                  pl.BlockSpec((2 * PEER_HEADS, N_KEYS, tm), lambda i, j: (0, 0, i)),
                  pl.BlockSpec((PEER_HEADS, tm), lambda i, j: (0, i)),
                  pl.BlockSpec((te, D_MODEL), lambda i, j: (j, 0)),
                  pl.BlockSpec((D_MODEL, te), lambda i, j: (0, j))],
        out_specs=pl.BlockSpec((tm, D_MODEL), lambda i, j: (i, 0)),
        out_shape=jax.ShapeDtypeStruct((n, D_MODEL), F32),
        scratch_shapes=[pltpu.VMEM((D_MODEL, tm), F32), pltpu.VMEM((te, tm), F32), pltpu.VMEM((te, tm), BF16)],
        compiler_params=_cparams(("parallel", "arbitrary")),
        name="peer",
    )(xn, st, e, tau, u, vt)


def _ple_kernel(h1_ref, po_ref, p_ref, g_ref, wg_ref, bg_ref, wp_ref, gf_ref, out_ref, *, final):
    h2 = h1_ref[...] + po_ref[...]
    gate = jax.nn.sigmoid(_dot(_rms(h2, g_ref[...]).astype(BF16), wg_ref[...]) + bg_ref[...])
    h3 = h2 + gate * _dot(p_ref[...].astype(BF16), wp_ref[...])
    out_ref[...] = _rms(h3, gf_ref[...]) if final else h3


def _ple(h1, po, p, g, wg, bg, wp, gf, final):
    n = h1.shape[0]
    tm = _tile(n, 512)
    kern = functools.partial(_ple_kernel, final=final)
    row = lambda i: (i, 0)
    fix = lambda i: (0, 0)
    return pl.pallas_call(
        kern,
        grid=(n // tm,),
        in_specs=[pl.BlockSpec((tm, D_MODEL), row), pl.BlockSpec((tm, D_MODEL), row),
                  pl.BlockSpec((tm, PLE_DIM), row), pl.BlockSpec((1, D_MODEL), fix),
                  pl.BlockSpec((D_MODEL, D_MODEL), fix), pl.BlockSpec((1, D_MODEL), fix),
                  pl.BlockSpec((PLE_DIM, D_MODEL), fix), pl.BlockSpec((1, D_MODEL), fix)],
        out_specs=pl.BlockSpec((tm, D_MODEL), row),
        out_shape=jax.ShapeDtypeStruct((n, D_MODEL), F32),
        compiler_params=_cparams(("parallel",)),
        name="ple",
    )(h1, po, p, g, wg, bg, wp, gf)


def _pad_lanes(x, width=LANES):
    pad = [(0, 0)] * (x.ndim - 1) + [(0, width - x.shape[-1])]
    return jnp.pad(x, pad)


def _trunk(h, p_all, groups, states, prm):
    (norm_mix, w_in_p, wup_p, b_gla_gate, gla_norm, conv_w, alog_row, dt_row, gdn_norm, w_out_b,
     norm_ffn, wq_hi, wq_lo, keys, u_b, vt_b, norm_ple, w_ple_gate_b, b_ple_gate, w_ple_proj_b, norm_final) = prm
    new_states = [([], [], []) for _ in groups]
    depth = w_in_p.shape[0]
    for i in range(depth):
        gla_in, gdn_in = _in_proj(h, norm_mix[i][None], w_in_p[i])
        o1s, o2s = [], []
        for gi, (B, L, row0, bb) in enumerate(groups):
            s_gla, s_gdn, s_conv = states[gi]
            o1, sg = _gla(gla_in, s_gla[i].reshape(B, 2, LANES, GLA_DV), wup_p[i], b_gla_gate[i][None],
                          gla_norm[i][None], B=B, L=L, row0=row0, bb=bb)
            o2, sd, sc = _gdn(gdn_in, s_gdn[i], s_conv[i], conv_w[i], alog_row[i], dt_row[i], gdn_norm[i][None],
                              B=B, L=L, row0=row0, bb=bb)
            o1s.append(o1)
            o2s.append(o2)
            new_states[gi][0].append(sg.reshape(B, GLA_HEADS, GLA_DK, GLA_DV))
            new_states[gi][1].append(sd)
            new_states[gi][2].append(sc)
        o1 = jnp.concatenate(o1s, axis=0)
        o2 = jnp.concatenate(o2s, axis=0)
        h1, xn, st = _mid(h, o1, o2, w_out_b[i], norm_ffn[i][None], wq_hi[i], wq_lo[i], keys[i])
        e, tau = _topk(st)
        po = _peer(xn, st, e, tau, u_b[i], vt_b[i])
        h = _ple(h1, po, p_all[i], norm_ple[i][None], w_ple_gate_b[i], b_ple_gate[i][None], w_ple_proj_b[i],
                 norm_final[None], final=(i == depth - 1))
    return h, [tuple(jnp.stack(x) for x in ns) for ns in new_states]


def kernel(x_prompt, x_sample, state_gla, state_gdn, state_conv, p_prompt, p_sample, norm_mix, w_in, w_gla_up, b_gla_gate, gla_norm, conv_w, gdn_a_log, gdn_dt_bias, gdn_norm, w_out, norm_ffn, w_peer_q, peer_sub_keys, peer_u, peer_v, norm_ple, w_ple_gate, b_ple_gate, w_ple_proj, norm_final):
    Bp, Lp, _ = x_prompt.shape
    Bs, Ls, _ = x_sample.shape
    n_p, n_s = Bp * Lp, Bs * Ls
    depth = w_in.shape[0]

    c0 = 2 * GLA_KW + 2 * GLA_WIDTH
    c1 = c0 + GLA_RANK
    c2 = c1 + 3 * GDN_WIDTH + GDN_WIDTH
    w_in_p = jnp.concatenate([w_in[..., :c0], _pad_lanes(w_in[..., c0:c1]), w_in[..., c1:c2],
                              _pad_lanes(w_in[..., c2:])], axis=-1).astype(BF16)
    wup_p = jnp.pad(w_gla_up, ((0, 0), (0, LANES - GLA_RANK), (0, 0)))
    lane_pad = ((0, 0), (GDN_HEADS, LANES - 2 * GDN_HEADS))
    alog_row = jnp.pad(gdn_a_log, lane_pad)[:, None, :]
    dt_row = jnp.pad(gdn_dt_bias, lane_pad)[:, None, :]
    wq_hi = w_peer_q.astype(BF16)
    wq_lo = (w_peer_q - wq_hi.astype(F32)).astype(BF16)
    keys = peer_sub_keys.reshape(depth, 2 * PEER_HEADS, N_KEYS, PEER_HALF)
    u_b = peer_u.astype(BF16)
    vt_b = jnp.swapaxes(peer_v, 1, 2).astype(BF16)
    prm = (norm_mix, w_in_p, wup_p, b_gla_gate, gla_norm, conv_w, alog_row, dt_row, gdn_norm, w_out.astype(BF16),
           norm_ffn, wq_hi, wq_lo, keys, u_b, vt_b, norm_ple, w_ple_gate.astype(BF16), b_ple_gate,
           w_ple_proj.astype(BF16), norm_final)

    cp = min(CHUNK, Lp)
    ncp = Lp // cp
    xp = x_prompt.reshape(Bp, ncp, cp, D_MODEL).transpose(1, 0, 2, 3).reshape(n_p, D_MODEL)
    pp = p_prompt.reshape(depth, Bp, ncp, cp, PLE_DIM).transpose(0, 2, 1, 3, 4).reshape(depth, n_p, PLE_DIM)
    h = jnp.concatenate([xp, x_sample.reshape(n_s, D_MODEL)], axis=0)
    p_all = jnp.concatenate([pp, p_sample.reshape(depth, n_s, PLE_DIM)], axis=1)
    zeros = lambda *s: jnp.zeros(s, F32)
    st_p = (zeros(depth, Bp, GLA_HEADS, GLA_DK, GLA_DV), zeros(depth, Bp, GDN_HEADS, GDN_DK, GDN_DV),
            zeros(depth, Bp, CONV_W - 1, 3 * GDN_WIDTH))
    st_s = (state_gla, state_gdn, state_conv)
    groups = [(Bp, Lp, 0, Bp), (Bs, Ls, n_p, min(Bs, SAMPLE_SEQS_PER_STEP))]
    y, (ns_p, ns_s) = _trunk(h, p_all, groups, [st_p, st_s], prm)
    y_prompt = y[:n_p].reshape(ncp, Bp, cp, D_MODEL).transpose(1, 0, 2, 3).reshape(Bp, Lp, D_MODEL)
    y_sample = y[n_p:].reshape(Bs, Ls, D_MODEL)
    return (y_prompt, y_sample, ns_p[0], ns_p[1], ns_p[2], ns_s[0], ns_s[1], ns_s[2])
```

```python
import functools

import jax
import jax.numpy as jnp
from jax import lax
from jax.experimental import pallas as pl
from jax.experimental.pallas import tpu as pltpu

F32 = jnp.float32
BF16 = jnp.bfloat16
HI = lax.Precision.HIGHEST

D_MODEL = 1024
DEPTH = 4
GLA_HEADS = 4
GLA_WIDTH = 512
GLA_DV = 128
GLA_DK = 64
GLA_KW = 256
GLA_RANK = 16
GLA_GATE_NORM = 16.0
GDN_HEADS = 4
GDN_WIDTH = 512
GDN_DK = 128
GDN_DV = 128
CONV_W = 4
CHUNK = 64
PEER_HEADS = 8
PEER_HALF = 128
N_KEYS = 128
N_EXPERTS = N_KEYS * N_KEYS
PEER_TOPK = 16
PLE_DIM = 256
EPS = 1e-6

LANES = 128
SUBLANES = 8
GLA_IN = 2 * GLA_KW + 2 * GLA_WIDTH + LANES
GDN_IN = 3 * GDN_WIDTH + GDN_WIDTH + LANES
GLA_SUB = 16
SAMPLE_SEQS_PER_STEP = 16
VMEM_LIMIT = 56 * 1024 * 1024


def _tile(n, pref):
    t = pref
    while n % t:
        t -= LANES
    return t


def _cparams(sem):
    return pltpu.CompilerParams(dimension_semantics=sem, vmem_limit_bytes=VMEM_LIMIT)


def _dot(a, b, prec=None):
    return jnp.dot(a, b, precision=prec, preferred_element_type=F32)


def _dot_nt(a, b, prec=None):
    return lax.dot_general(a, b, (((1,), (1,)), ((), ())), precision=prec, preferred_element_type=F32)


def _bmm(a, b, prec=None):
    return lax.dot_general(a, b, (((2,), (1,)), ((0,), (0,))), precision=prec, preferred_element_type=F32)


def _bmm_nt(a, b, prec=None):
    return lax.dot_general(a, b, (((2,), (2,)), ((0,), (0,))), precision=prec, preferred_element_type=F32)


def _bmm_tn(a, b, prec=None):
    return lax.dot_general(a, b, (((1,), (1,)), ((0,), (0,))), precision=prec, preferred_element_type=F32)


def _split(a):
    hi = a.astype(BF16)
    return hi, (a - hi.astype(F32)).astype(BF16)


def _bmm3(a, b, nt=False):
    f = _bmm_nt if nt else _bmm
    return f(a[0], b[0]) + (f(a[0], b[1]) + f(a[1], b[0]))


def _rms(x, g):
    return x * lax.rsqrt(jnp.mean(x * x, axis=-1, keepdims=True) + EPS) * g


def _softplus(x):
    return jnp.maximum(x, 0.0) + jnp.log1p(jnp.exp(-jnp.abs(x)))


def _silu(x):
    return x * jax.nn.sigmoid(x)


def _in_proj_kernel(h_ref, g_ref, w_ref, gla_ref, gdn_ref):
    xn = _rms(h_ref[...], g_ref[...]).astype(BF16)
    gla_ref[...] = _dot(xn, w_ref[:, :GLA_IN])
    gdn_ref[...] = _dot(xn, w_ref[:, GLA_IN:])


def _in_proj(h, g, w):
    n = h.shape[0]
    tm = _tile(n, 256)
    return pl.pallas_call(
        _in_proj_kernel,
        grid=(n // tm,),
        in_specs=[pl.BlockSpec((tm, D_MODEL), lambda i: (i, 0)),
                  pl.BlockSpec((1, D_MODEL), lambda i: (0, 0)),
                  pl.BlockSpec((D_MODEL, GLA_IN + GDN_IN), lambda i: (0, 0))],
        out_specs=[pl.BlockSpec((tm, GLA_IN), lambda i: (i, 0)),
                   pl.BlockSpec((tm, GDN_IN), lambda i: (i, 0))],
        out_shape=[jax.ShapeDtypeStruct((n, GLA_IN), F32), jax.ShapeDtypeStruct((n, GDN_IN), F32)],
        compiler_params=_cparams(("parallel",)),
        name="in_proj",
    )(h, g, w)


def _gla_kernel(x_ref, s0_ref, wup_ref, bg_ref, gn_ref, o_ref, s_ref, *, C, SB, bb):
    c = pl.program_id(1)

    @pl.when(c == 0)
    def _():
        s_ref[...] = s0_ref[...]

    nb = C // SB
    G = 2 * bb
    lane = lax.broadcasted_iota(jnp.int32, (1, 1, LANES), 2)
    hmask = (lane < GLA_DK, lane >= GLA_DK)
    tri = (lax.broadcasted_iota(jnp.int32, (bb, C, C), 1) >= lax.broadcasted_iota(jnp.int32, (bb, C, C), 2)).astype(F32)
    t_sb = lax.broadcasted_iota(jnp.int32, (1, SB, 1), 1)
    col_c = lax.broadcasted_iota(jnp.int32, (1, SB, C), 2)
    eye = lax.broadcasted_iota(jnp.int32, (1, LANES, LANES), 1) == lax.broadcasted_iota(jnp.int32, (1, LANES, LANES), 2)
    row_l = lax.broadcasted_iota(jnp.int32, (1, LANES, 1), 1)
    c0 = 2 * GLA_KW + 2 * GLA_WIDTH

    def pairs(a, off, width):
        parts = [a[:, off + p * width:off + (p + 1) * width].reshape(bb, 1, C, width) for p in range(2)]
        return jnp.concatenate(parts, axis=1).reshape(G, C, width)

    z = _dot(x_ref[:, c0:], wup_ref[...], HI) + bg_ref[...]
    gk = (jnp.minimum(z, 0.0) - jnp.log1p(jnp.exp(-jnp.abs(z)))) * (1.0 / GLA_GATE_NORM)
    b_all = _bmm(tri, gk.reshape(bb, C, GLA_KW), HI).reshape(bb * C, GLA_KW)
    q = pairs(x_ref[:, :GLA_KW], 0, LANES) * (GLA_DK ** -0.5)
    k = pairs(x_ref[:, GLA_KW:2 * GLA_KW], 0, LANES)
    b = pairs(b_all, 0, LANES)
    v = pairs(x_ref[:, 2 * GLA_KW:2 * GLA_KW + GLA_WIDTH], 0, 2 * GLA_DV)
    r = pairs(x_ref[:, 2 * GLA_KW + GLA_WIDTH:c0], 0, 2 * GLA_DV)
    S = s_ref[...].reshape(G, LANES, GLA_DV)

    o_blocks = ([], [])
    for i in range(nb):
        bi, qi, ki = b[:, i * SB:(i + 1) * SB], q[:, i * SB:(i + 1) * SB], k[:, i * SB:(i + 1) * SB]
        att_d = [jnp.zeros((G, SB, C), F32), jnp.zeros((G, SB, C), F32)]
        for j in range(SB):
            e = jnp.exp(jnp.where(t_sb >= j, bi - bi[:, j:j + 1], 0.0))
            term = qi * ki[:, j:j + 1] * e
            for h in range(2):
                col = jnp.sum(jnp.where(hmask[h], term, 0.0), axis=-1, keepdims=True)
                att_d[h] = jnp.where(col_c == i * SB + j, jnp.where(t_sb >= j, col, 0.0), att_d[h])
        if i > 0:
            ri = b[:, i * SB - 1:i * SB]
            qd = qi * jnp.exp(bi - ri)
            kd = k * jnp.exp(jnp.minimum(ri - b, 0.0))
        for h in range(2):
            att = att_d[h]
            if i > 0:
                att = jnp.where(col_c < i * SB, _bmm_nt(jnp.where(hmask[h], qd, 0.0), kd), att)
            o_blocks[h].append(_bmm(att, v[:, :, h * GLA_DV:(h + 1) * GLA_DV]))

    qe = q * jnp.exp(b)
    b_last = b[:, C - 1:C]
    upd = _bmm_tn(k * jnp.exp(b_last - b), v)
    dec = jnp.where(eye, jnp.broadcast_to(jnp.exp(b_last), (G, LANES, LANES)), 0.0)
    s_new = _bmm(dec, S, HI) + jnp.where(row_l < GLA_DK, upd[:, :, :GLA_DV], upd[:, :, GLA_DV:])
    s_ref[...] = s_new.reshape(bb, 2, LANES, GLA_DV)
    for h in range(2):
        o_intra = o_blocks[h][0] if nb == 1 else jnp.concatenate(o_blocks[h], axis=1)
        o = o_intra + _bmm(jnp.where(hmask[h], qe, 0.0), S)
        y = (_rms(o, gn_ref[...]) * _silu(r[:, :, h * GLA_DV:(h + 1) * GLA_DV])).reshape(bb, 2, C, GLA_DV)
        for p in range(2):
            hh = 2 * p + h
            o_ref[:, hh * GLA_DV:(hh + 1) * GLA_DV] = y[:, p].reshape(bb * C, GLA_DV)


def _gla(x, s0, wup, bg, gn, *, B, L, row0, bb):
    C = min(CHUNK, L)
    SB = min(GLA_SUB, C)
    nc = L // C
    rb = bb * C
    blk0 = row0 // rb
    kern = functools.partial(_gla_kernel, C=C, SB=SB, bb=bb)
    return pl.pallas_call(
        kern,
        grid=(B // bb, nc),
        in_specs=[pl.BlockSpec((rb, GLA_IN), lambda b, c: (blk0 + b * nc + c, 0)),
                  pl.BlockSpec((bb, 2, LANES, GLA_DV), lambda b, c: (b, 0, 0, 0)),
                  pl.BlockSpec((LANES, GLA_KW), lambda b, c: (0, 0)),
                  pl.BlockSpec((1, GLA_KW), lambda b, c: (0, 0)),
                  pl.BlockSpec((1, GLA_DV), lambda b, c: (0, 0))],
        out_specs=[pl.BlockSpec((rb, GLA_WIDTH), lambda b, c: (b * nc + c, 0)),
                   pl.BlockSpec((bb, 2, LANES, GLA_DV), lambda b, c: (b, 0, 0, 0))],
        out_shape=[jax.ShapeDtypeStruct((B * L, GLA_WIDTH), F32),
                   jax.ShapeDtypeStruct((B, 2, LANES, GLA_DV), F32)],
        compiler_params=_cparams(("parallel", "arbitrary")),
        name="gla",
    )(x, s0, wup, bg, gn)


def _gdn_kernel(x_ref, s0_ref, cb_ref, cw_ref, alog_ref, dt_ref, gn_ref, o_ref, s_ref, cn_ref, xbuf, *, C, bb):
    c = pl.program_id(1)
    W = 3 * GDN_WIDTH
    P0 = SUBLANES - (CONV_W - 1)
    G = bb * GDN_HEADS

    @pl.when(c == 0)
    def _():
        s_ref[...] = s0_ref[...]
        xbuf[:, P0:SUBLANES, :] = cb_ref[...]

    ri = lax.broadcasted_iota(jnp.int32, (1, C, C), 1)
    ci = lax.broadcasted_iota(jnp.int32, (1, C, C), 2)
    incl = ri >= ci
    strict = ri > ci
    eye = (ri == ci).astype(F32)
    tri = jnp.broadcast_to(incl.astype(F32), (bb, C, C))
    ones = jnp.ones((G, C, C), F32)
    n_sq = max(C.bit_length() - 2, 0)

    def heads(a, off, width):
        parts = [a[:, :, off + h * width:off + (h + 1) * width].reshape(bb, 1, C, width) for h in range(GDN_HEADS)]
        return jnp.concatenate(parts, axis=1).reshape(G, C, width)

    xbuf[:, SUBLANES:SUBLANES + C, :] = x_ref[:, :W].reshape(bb, C, W)
    conv = xbuf[:, P0:P0 + C, :] * cw_ref[0:1, :]
    for j in range(1, CONV_W):
        conv = conv + xbuf[:, P0 + j:P0 + j + C, :] * cw_ref[j:j + 1, :]
    tail = xbuf[:, SUBLANES + C - (CONV_W - 1):SUBLANES + C, :]
    xbuf[:, P0:SUBLANES, :] = tail
    cn_ref[...] = tail
    qkv = _silu(conv)

    small = x_ref[:, W + GDN_WIDTH:].reshape(bb, C, LANES)
    beta_all = jax.nn.sigmoid(small)
    g_all = -jnp.exp(alog_ref[...]) * _softplus(small + dt_ref[...])
    b_all = _bmm(tri, g_all, HI)

    qh = heads(qkv, 0, GDN_DK)
    kh = heads(qkv, GDN_WIDTH, GDN_DK)
    vh = heads(qkv, 2 * GDN_WIDTH, GDN_DV)
    qh = qh * lax.rsqrt(jnp.sum(qh * qh, axis=-1, keepdims=True) + EPS) * (GDN_DK ** -0.5)
    kh = kh * lax.rsqrt(jnp.sum(kh * kh, axis=-1, keepdims=True) + EPS)
    beta = heads(beta_all, 0, 1)
    gcol = heads(g_all, GDN_HEADS, 1)
    bcol = heads(b_all, GDN_HEADS, 1)
    brow = _bmm(ones, jnp.where(ri <= ci, gcol, 0.0), HI)
    decay = jnp.where(incl, jnp.exp(jnp.where(incl, bcol - brow, 0.0)), 0.0)
    kb = kh * beta
    nm = -jnp.where(strict, _bmm3(_split(kb), _split(kh), nt=True) * decay, 0.0)
    tinv = eye + nm
    pw = nm
    for _ in range(n_sq):
        pws = _split(pw)
        pw = _bmm3(pws, pws)
        tinv = tinv + _bmm3(_split(tinv), _split(pw))
    ebc = jnp.exp(bcol)
    tis = _split(tinv)
    u = _bmm3(tis, _split(vh * beta))
    w = _bmm3(tis, _split(kb * ebc))
    S = s_ref[...].reshape(G, GDN_DK, GDN_DV)
    v_new = u - _bmm(w, S)
    qk = _bmm_nt(qh, kh) * decay
    o = _bmm(qh * ebc, S) + _bmm(qk, v_new)
    b_last = bcol[:, C - 1:C]
    s_new = jnp.exp(b_last) * S + _bmm_tn(kh * jnp.exp(b_last - bcol), v_new)
    s_ref[...] = s_new.reshape(bb, GDN_HEADS, GDN_DK, GDN_DV)
    zg = heads(x_ref[:, W:W + GDN_WIDTH].reshape(bb, C, GDN_WIDTH), 0, GDN_DV)
    y = (_rms(o, gn_ref[...]) * _silu(zg)).reshape(bb, GDN_HEADS, C, GDN_DV)
    for h in range(GDN_HEADS):
        o_ref[:, h * GDN_DV:(h + 1) * GDN_DV] = y[:, h].reshape(bb * C, GDN_DV)


def _gdn(x, s0, cbuf, cw, alog, dt, gn, *, B, L, row0, bb):
    C = min(CHUNK, L)
    nc = L // C
    rb = bb * C
    blk0 = row0 // rb
    W = 3 * GDN_WIDTH
    kern = functools.partial(_gdn_kernel, C=C, bb=bb)
    return pl.pallas_call(
        kern,
        grid=(B // bb, nc),
        in_specs=[pl.BlockSpec((rb, GDN_IN), lambda b, c: (blk0 + b * nc + c, 0)),
                  pl.BlockSpec((bb, GDN_HEADS, GDN_DK, GDN_DV), lambda b, c: (b, 0, 0, 0)),
                  pl.BlockSpec((bb, CONV_W - 1, W), lambda b, c: (b, 0, 0)),
                  pl.BlockSpec((CONV_W, W), lambda b, c: (0, 0)),
                  pl.BlockSpec((1, LANES), lambda b, c: (0, 0)),
                  pl.BlockSpec((1, LANES), lambda b, c: (0, 0)),
                  pl.BlockSpec((1, GDN_DV), lambda b, c: (0, 0))],
        out_specs=[pl.BlockSpec((rb, GDN_WIDTH), lambda b, c: (b * nc + c, 0)),
                   pl.BlockSpec((bb, GDN_HEADS, GDN_DK, GDN_DV), lambda b, c: (b, 0, 0, 0)),
                   pl.BlockSpec((bb, CONV_W - 1, W), lambda b, c: (b, 0, 0))],
        out_shape=[jax.ShapeDtypeStruct((B * L, GDN_WIDTH), F32),
                   jax.ShapeDtypeStruct((B, GDN_HEADS, GDN_DK, GDN_DV), F32),
                   jax.ShapeDtypeStruct((B, CONV_W - 1, W), F32)],
        scratch_shapes=[pltpu.VMEM((bb, SUBLANES + C, W), F32)],
        compiler_params=_cparams(("parallel", "arbitrary")),
        name="gdn",
    )(x, s0, cbuf, cw, alog, dt, gn)


def _mid_kernel(h_ref, o1_ref, o2_ref, wo_ref, g_ref, wqh_ref, wql_ref, keys_ref, h1_ref, xn_ref, st_ref):
    h1 = (h_ref[...] + _dot(o1_ref[...].astype(BF16), wo_ref[:GLA_WIDTH, :])
          + _dot(o2_ref[...].astype(BF16), wo_ref[GLA_WIDTH:, :]))
    h1_ref[...] = h1
    xn = _rms(h1, g_ref[...])
    x_hi = xn.astype(BF16)
    xn_ref[...] = x_hi
    x_lo = (xn - x_hi.astype(F32)).astype(BF16)
    q = _dot(x_hi, wqh_ref[...]) + (_dot(x_hi, wql_ref[...]) + _dot(x_lo, wqh_ref[...]))
    for hc in range(2 * PEER_HEADS):
        st_ref[hc] = _dot_nt(keys_ref[hc], q[:, hc * PEER_HALF:(hc + 1) * PEER_HALF], HI)


def _mid(h, o1, o2, wo, g, wqh, wql, keys):
    n = h.shape[0]
    tm = _tile(n, 256)
    nq = 2 * PEER_HEADS * PEER_HALF
    return pl.pallas_call(
        _mid_kernel,
        grid=(n // tm,),
        in_specs=[pl.BlockSpec((tm, D_MODEL), lambda i: (i, 0)),
                  pl.BlockSpec((tm, GLA_WIDTH), lambda i: (i, 0)),
                  pl.BlockSpec((tm, GDN_WIDTH), lambda i: (i, 0)),
                  pl.BlockSpec((D_MODEL, D_MODEL), lambda i: (0, 0)),
                  pl.BlockSpec((1, D_MODEL), lambda i: (0, 0)),
                  pl.BlockSpec((D_MODEL, nq), lambda i: (0, 0)),
                  pl.BlockSpec((D_MODEL, nq), lambda i: (0, 0)),
                  pl.BlockSpec((2 * PEER_HEADS, N_KEYS, PEER_HALF), lambda i: (0, 0, 0))],
        out_specs=[pl.BlockSpec((tm, D_MODEL), lambda i: (i, 0)),
                   pl.BlockSpec((tm, D_MODEL), lambda i: (i, 0)),
                   pl.BlockSpec((2 * PEER_HEADS, N_KEYS, tm), lambda i: (0, 0, i))],
        out_shape=[jax.ShapeDtypeStruct((n, D_MODEL), F32),
                   jax.ShapeDtypeStruct((n, D_MODEL), BF16),
                   jax.ShapeDtypeStruct((2 * PEER_HEADS, N_KEYS, n), F32)],
        compiler_params=_cparams(("parallel",)),
        name="mid",
    )(h, o1, o2, wo, g, wqh, wql, keys)


CAND_ROWS = 80


def _top16(S, io, n):
    vals = []
    for k in range(PEER_TOPK):
        m = jnp.max(S, axis=0, keepdims=True)
        vals.append(m)
        if k + 1 < PEER_TOPK:
            idx = jnp.min(jnp.where(S == m, io, n), axis=0, keepdims=True)
            S = jnp.where(io == idx, -jnp.inf, S)
    return vals


def _topk_kernel(s_ref, e_ref, tau_ref, v_ref, c_ref, *, tm):
    io = lax.broadcasted_iota(jnp.int32, (N_KEYS, tm), 0)
    io_c = lax.broadcasted_iota(jnp.int32, (CAND_ROWS, tm), 0)
    io8 = lax.broadcasted_iota(jnp.int32, (SUBLANES, tm), 0)

    def head(h, carry):
        for c in range(2):
            vals = _top16(s_ref[2 * h + c], io, N_KEYS)
            for k in range(PEER_TOPK):
                v_ref[c, k:k + 1, :] = vals[k]
        c_ref[0:16, :] = v_ref[0, 0:1, :] + v_ref[1, 0:16, :]
        off = 16
        for i in range(1, 8):
            nj = PEER_TOPK // (i + 1)
            c_ref[off:off + SUBLANES, :] = jnp.where(io8 < nj, v_ref[0, i:i + 1, :] + v_ref[1, 0:SUBLANES, :], -jnp.inf)
            off += SUBLANES
        c_ref[off:off + SUBLANES, :] = v_ref[0, 8:16, :] + v_ref[1, 0:1, :]
        cv = _top16(c_ref[...], io_c, CAND_ROWS)
        z = jnp.ones_like(cv[0])
        for k in range(1, PEER_TOPK):
            z = z + jnp.exp(cv[k] - cv[0])
        tau_ref[pl.ds(h, 1), :] = cv[PEER_TOPK - 1]
        s1 = s_ref[2 * h]
        s2 = s_ref[2 * h + 1]
        e_ref[2 * h] = jnp.exp(s1 - v_ref[0, 0:1, :]) * (1.0 / z)
        e_ref[2 * h + 1] = jnp.exp(s2 - v_ref[1, 0:1, :])
        return carry

    for h in range(PEER_HEADS):
        head(h, 0)


def _topk(st):
    n = st.shape[-1]
    tm = _tile(n, 128)
    kern = functools.partial(_topk_kernel, tm=tm)
    return pl.pallas_call(
        kern,
        grid=(n // tm,),
        in_specs=[pl.BlockSpec((2 * PEER_HEADS, N_KEYS, tm), lambda i: (0, 0, i))],
        out_specs=[pl.BlockSpec((2 * PEER_HEADS, N_KEYS, tm), lambda i: (0, 0, i)),
                   pl.BlockSpec((PEER_HEADS, tm), lambda i: (0, i))],
        out_shape=[jax.ShapeDtypeStruct((2 * PEER_HEADS, N_KEYS, n), F32),
                   jax.ShapeDtypeStruct((PEER_HEADS, n), F32)],
        scratch_shapes=[pltpu.VMEM((2, PEER_TOPK, tm), F32), pltpu.VMEM((CAND_ROWS, tm), F32)],
        compiler_params=_cparams(("parallel",)),
        name="topk",
    )(st)


def _peer_kernel(xn_ref, s_ref, e_ref, tau_ref, u_ref, vt_ref, out_ref, acc_ref, st_ref, p_ref, *, te, tm):
    j = pl.program_id(1)
    n1 = te // N_KEYS

    @pl.when(j == 0)
    def _():
        acc_ref[...] = jnp.zeros_like(acc_ref)

    st_ref[...] = _dot_nt(u_ref[...], xn_ref[...])

    def body(jj, carry):
        i1 = j * n1 + jj
        g = jnp.zeros((N_KEYS, tm), F32)
        for h in range(PEER_HEADS):
            t = s_ref[2 * h, pl.ds(i1, 1), :] + s_ref[2 * h + 1]
            w = e_ref[2 * h, pl.ds(i1, 1), :] * e_ref[2 * h + 1]
            g = g + jnp.where(t >= tau_ref[h:h + 1, :], w, 0.0)
        r = pl.multiple_of(jj * N_KEYS, N_KEYS)
        hid = st_ref[pl.ds(r, N_KEYS), :]
        act = 0.5 * hid * (1.0 + lax.erf(hid * (0.5 ** 0.5)))
        p_ref[pl.ds(r, N_KEYS), :] = (act * g).astype(BF16)
        return carry

    lax.fori_loop(0, n1, body, 0)
    acc_ref[...] += _dot(vt_ref[...], p_ref[...])

    @pl.when(j == pl.num_programs(1) - 1)
    def _():
        out_ref[...] = acc_ref[...].T


def _peer(xn, st, e, tau, u, vt, te=2048):
    n = xn.shape[0]
    tm = _tile(n, 512)
    kern = functools.partial(_peer_kernel, te=te, tm=tm)
    return pl.pallas_call(
        kern,
        grid=(n // tm, N_EXPERTS // te),
        in_specs=[pl.BlockSpec((tm, D_MODEL), lambda i, j: (i, 0)),
                  pl.BlockSpec((2 * PEER_HEADS, N_KEYS, tm), lambda i, j: (0, 0, i)),
                  pl.BlockSpec((2 * PEER_HEADS, N_KEYS, tm), lambda i, j: (0, 0, i)),
                  pl.BlockSpec((PEER_HEADS, tm), lambda i, j: (0, i)),
                  pl.BlockSpec((te, D_MODEL), lambda i, j: (j, 0)),
                  pl.BlockSpec((D_MODEL, te), lambda i, j: (0, j))],
        out_specs=pl.BlockSpec((tm, D_MODEL), lambda i, j: (i, 0)),
        out_shape=jax.ShapeDtypeStruct((n, D_MODEL), F32),
        scratch_shapes=[pltpu.VMEM((D_MODEL, tm), F32), pltpu.VMEM((te, tm), F32), pltpu.VMEM((te, tm), BF16)],
        compiler_params=_cparams(("parallel", "arbitrary")),
        name="peer",
    )(xn, st, e, tau, u, vt)


def _ple_kernel(h1_ref, po_ref, p_ref, g_ref, wg_ref, bg_ref, wp_ref, gf_ref, out_ref, *, final):
    h2 = h1_ref[...] + po_ref[...]
    gate = jax.nn.sigmoid(_dot(_rms(h2, g_ref[...]).astype(BF16), wg_ref[...]) + bg_ref[...])
    h3 = h2 + gate * _dot(p_ref[...].astype(BF16), wp_ref[...])
    out_ref[...] = _rms(h3, gf_ref[...]) if final else h3


def _ple(h1, po, p, g, wg, bg, wp, gf, final):
    n = h1.shape[0]
    tm = _tile(n, 512)
    kern = functools.partial(_ple_kernel, final=final)
    row = lambda i: (i, 0)
    fix = lambda i: (0, 0)
    return pl.pallas_call(
        kern,
        grid=(n // tm,),
        in_specs=[pl.BlockSpec((tm, D_MODEL), row), pl.BlockSpec((tm, D_MODEL), row),
                  pl.BlockSpec((tm, PLE_DIM), row), pl.BlockSpec((1, D_MODEL), fix),
                  pl.BlockSpec((D_MODEL, D_MODEL), fix), pl.BlockSpec((1, D_MODEL), fix),
                  pl.BlockSpec((PLE_DIM, D_MODEL), fix), pl.BlockSpec((1, D_MODEL), fix)],
        out_specs=pl.BlockSpec((tm, D_MODEL), row),
        out_shape=jax.ShapeDtypeStruct((n, D_MODEL), F32),
        compiler_params=_cparams(("parallel",)),
        name="ple",
    )(h1, po, p, g, wg, bg, wp, gf)


def _pad_lanes(x, width=LANES):
    pad = [(0, 0)] * (x.ndim - 1) + [(0, width - x.shape[-1])]
    return jnp.pad(x, pad)


def _trunk(h, p_all, groups, states, prm):
    (norm_mix, w_in_p, wup_p, b_gla_gate, gla_norm, conv_w, alog_row, dt_row, gdn_norm, w_out_b,
     norm_ffn, wq_hi, wq_lo, keys, u_b, vt_b, norm_ple, w_ple_gate_b, b_ple_gate, w_ple_proj_b, norm_final) = prm
    new_states = [([], [], []) for _ in groups]
    depth = w_in_p.shape[0]
    for i in range(depth):
        gla_in, gdn_in = _in_proj(h, norm_mix[i][None], w_in_p[i])
        o1s, o2s = [], []
        for gi, (B, L, row0, bb) in enumerate(groups):
            s_gla, s_gdn, s_conv = states[gi]
            o1, sg = _gla(gla_in, s_gla[i].reshape(B, 2, LANES, GLA_DV), wup_p[i], b_gla_gate[i][None],
                          gla_norm[i][None], B=B, L=L, row0=row0, bb=bb)
            o2, sd, sc = _gdn(gdn_in, s_gdn[i], s_conv[i], conv_w[i], alog_row[i], dt_row[i], gdn_norm[i][None],
                              B=B, L=L, row0=row0, bb=bb)
            o1s.append(o1)
            o2s.append(o2)
            new_states[gi][0].append(sg.reshape(B, GLA_HEADS, GLA_DK, GLA_DV))
            new_states[gi][1].append(sd)
            new_states[gi][2].append(sc)
        o1 = jnp.concatenate(o1s, axis=0)
        o2 = jnp.concatenate(o2s, axis=0)
        h1, xn, st = _mid(h, o1, o2, w_out_b[i], norm_ffn[i][None], wq_hi[i], wq_lo[i], keys[i])
        e, tau = _topk(st)
        po = _peer(xn, st, e, tau, u_b[i], vt_b[i])
        h = _ple(h1, po, p_all[i], norm_ple[i][None], w_ple_gate_b[i], b_ple_gate[i][None], w_ple_proj_b[i],
                 norm_final[None], final=(i == depth - 1))
    return h, [tuple(jnp.stack(x) for x in ns) for ns in new_states]


def kernel(x_prompt, x_sample, state_gla, state_gdn, state_conv, p_prompt, p_sample, norm_mix, w_in, w_gla_up, b_gla_gate, gla_norm, conv_w, gdn_a_log, gdn_dt_bias, gdn_norm, w_out, norm_ffn, w_peer_q, peer_sub_keys, peer_u, peer_v, norm_ple, w_ple_gate, b_ple_gate, w_ple_proj, norm_final):
    Bp, Lp, _ = x_prompt.shape
    Bs, Ls, _ = x_sample.shape
    n_p, n_s = Bp * Lp, Bs * Ls
    depth = w_in.shape[0]

    c0 = 2 * GLA_KW + 2 * GLA_WIDTH
    c1 = c0 + GLA_RANK
    c2 = c1 + 3 * GDN_WIDTH + GDN_WIDTH
    w_in_p = jnp.concatenate([w_in[..., :c0], _pad_lanes(w_in[..., c0:c1]), w_in[..., c1:c2],
                              _pad_lanes(w_in[..., c2:])], axis=-1).astype(BF16)
    wup_p = jnp.pad(w_gla_up, ((0, 0), (0, LANES - GLA_RANK), (0, 0)))
    lane_pad = ((0, 0), (GDN_HEADS, LANES - 2 * GDN_HEADS))
    alog_row = jnp.pad(gdn_a_log, lane_pad)[:, None, :]
    dt_row = jnp.pad(gdn_dt_bias, lane_pad)[:, None, :]
    wq_hi = w_peer_q.astype(BF16)
    wq_lo = (w_peer_q - wq_hi.astype(F32)).astype(BF16)
    keys = peer_sub_keys.reshape(depth, 2 * PEER_HEADS, N_KEYS, PEER_HALF)
    u_b = peer_u.astype(BF16)
    vt_b = jnp.swapaxes(peer_v, 1, 2).astype(BF16)
    prm = (norm_mix, w_in_p, wup_p, b_gla_gate, gla_norm, conv_w, alog_row, dt_row, gdn_norm, w_out.astype(BF16),
           norm_ffn, wq_hi, wq_lo, keys, u_b, vt_b, norm_ple, w_ple_gate.astype(BF16), b_ple_gate,
           w_ple_proj.astype(BF16), norm_final)

    cp = min(CHUNK, Lp)
    ncp = Lp // cp
    xp = x_prompt.reshape(Bp, ncp, cp, D_MODEL).transpose(1, 0, 2, 3).reshape(n_p, D_MODEL)
    pp = p_prompt.reshape(depth, Bp, ncp, cp, PLE_DIM).transpose(0, 2, 1, 3, 4).reshape(depth, n_p, PLE_DIM)
    h = jnp.concatenate([xp, x_sample.reshape(n_s, D_MODEL)], axis=0)
    p_all = jnp.concatenate([pp, p_sample.reshape(depth, n_s, PLE_DIM)], axis=1)
    zeros = lambda *s: jnp.zeros(s, F32)
    st_p = (zeros(depth, Bp, GLA_HEADS, GLA_DK, GLA_DV), zeros(depth, Bp, GDN_HEADS, GDN_DK, GDN_DV),
            zeros(depth, Bp, CONV_W - 1, 3 * GDN_WIDTH))
    st_s = (state_gla, state_gdn, state_conv)
    groups = [(Bp, Lp, 0, Bp), (Bs, Ls, n_p, min(Bs, SAMPLE_SEQS_PER_STEP))]
    y, (ns_p, ns_s) = _trunk(h, p_all, groups, [st_p, st_s], prm)
    y_prompt = y[:n_p].reshape(ncp, Bp, cp, D_MODEL).transpose(1, 0, 2, 3).reshape(Bp, Lp, D_MODEL)
    y_sample = y[n_p:].reshape(Bs, Ls, D_MODEL)
    return (y_prompt, y_sample, ns_p[0], ns_p[1], ns_p[2], ns_s[0], ns_s[1], ns_s[2])
```

```python
import functools

import jax
import jax.numpy as jnp
from jax import lax
from jax.experimental import pallas as pl
from jax.experimental.pallas import tpu as pltpu

F32 = jnp.float32
BF16 = jnp.bfloat16
HI = lax.Precision.HIGHEST

D_MODEL = 1024
DEPTH = 4
GLA_HEADS = 4
GLA_WIDTH = 512
GLA_DV = 128
GLA_DK = 64
GLA_KW = 256
GLA_RANK = 16
GLA_GATE_NORM = 16.0
GDN_HEADS = 4
GDN_WIDTH = 512
GDN_DK = 128
GDN_DV = 128
CONV_W = 4
CHUNK = 64
PEER_HEADS = 8
PEER_HALF = 128
N_KEYS = 128
N_EXPERTS = N_KEYS * N_KEYS
PEER_TOPK = 16
PLE_DIM = 256
EPS = 1e-6

LANES = 128
SUBLANES = 8
GLA_IN = 2 * GLA_KW + 2 * GLA_WIDTH + LANES
GDN_IN = 3 * GDN_WIDTH + GDN_WIDTH + LANES
GLA_SUB = 16
SAMPLE_SEQS_PER_STEP = 16
VMEM_LIMIT = 56 * 1024 * 1024


def _tile(n, pref):
    t = pref
    while n % t:
        t -= LANES
    return t


def _cparams(sem):
    return pltpu.CompilerParams(dimension_semantics=sem, vmem_limit_bytes=VMEM_LIMIT)


def _dot(a, b, prec=None):
    return jnp.dot(a, b, precision=prec, preferred_element_type=F32)


def _dot_nt(a, b, prec=None):
    return lax.dot_general(a, b, (((1,), (1,)), ((), ())), precision=prec, preferred_element_type=F32)


def _bmm(a, b, prec=None):
    return lax.dot_general(a, b, (((2,), (1,)), ((0,), (0,))), precision=prec, preferred_element_type=F32)


def _bmm_nt(a, b, prec=None):
    return lax.dot_general(a, b, (((2,), (2,)), ((0,), (0,))), precision=prec, preferred_element_type=F32)


def _bmm_tn(a, b, prec=None):
    return lax.dot_general(a, b, (((1,), (1,)), ((0,), (0,))), precision=prec, preferred_element_type=F32)


def _split(a):
    hi = a.astype(BF16)
    return hi, (a - hi.astype(F32)).astype(BF16)


def _bmm3(a, b, nt=False):
    f = _bmm_nt if nt else _bmm
    return f(a[0], b[0]) + (f(a[0], b[1]) + f(a[1], b[0]))


def _rms(x, g):
    return x * lax.rsqrt(jnp.mean(x * x, axis=-1, keepdims=True) + EPS) * g


def _softplus(x):
    return jnp.maximum(x, 0.0) + jnp.log1p(jnp.exp(-jnp.abs(x)))


def _silu(x):
    return x * jax.nn.sigmoid(x)


def _in_proj_kernel(h_ref, g_ref, w_ref, gla_ref, gdn_ref):
    xn = _rms(h_ref[...], g_ref[...]).astype(BF16)
    gla_ref[...] = _dot(xn, w_ref[:, :GLA_IN])
    gdn_ref[...] = _dot(xn, w_ref[:, GLA_IN:])


def _in_proj(h, g, w):
    n = h.shape[0]
    tm = _tile(n, 256)
    return pl.pallas_call(
        _in_proj_kernel,
        grid=(n // tm,),
        in_specs=[pl.BlockSpec((tm, D_MODEL), lambda i: (i, 0)),
                  pl.BlockSpec((1, D_MODEL), lambda i: (0, 0)),
                  pl.BlockSpec((D_MODEL, GLA_IN + GDN_IN), lambda i: (0, 0))],
        out_specs=[pl.BlockSpec((tm, GLA_IN), lambda i: (i, 0)),
                   pl.BlockSpec((tm, GDN_IN), lambda i: (i, 0))],
        out_shape=[jax.ShapeDtypeStruct((n, GLA_IN), F32), jax.ShapeDtypeStruct((n, GDN_IN), F32)],
        compiler_params=_cparams(("parallel",)),
        name="in_proj",
    )(h, g, w)


def _gla_kernel(x_ref, s0_ref, wup_ref, bg_ref, gn_ref, o_ref, s_ref, *, C, SB, bb):
    c = pl.program_id(1)

    @pl.when(c == 0)
    def _():
        s_ref[...] = s0_ref[...]

    nb = C // SB
    G = 2 * bb
    lane = lax.broadcasted_iota(jnp.int32, (1, 1, LANES), 2)
    hmask = (lane < GLA_DK, lane >= GLA_DK)
    tri = (lax.broadcasted_iota(jnp.int32, (bb, C, C), 1) >= lax.broadcasted_iota(jnp.int32, (bb, C, C), 2)).astype(F32)
    t_sb = lax.broadcasted_iota(jnp.int32, (1, SB, 1), 1)
    col_c = lax.broadcasted_iota(jnp.int32, (1, SB, C), 2)
    eye = lax.broadcasted_iota(jnp.int32, (1, LANES, LANES), 1) == lax.broadcasted_iota(jnp.int32, (1, LANES, LANES), 2)
    row_l = lax.broadcasted_iota(jnp.int32, (1, LANES, 1), 1)
    c0 = 2 * GLA_KW + 2 * GLA_WIDTH

    def pairs(a, off, width):
        parts = [a[:, off + p * width:off + (p + 1) * width].reshape(bb, 1, C, width) for p in range(2)]
        return jnp.concatenate(parts, axis=1).reshape(G, C, width)

    z = _dot(x_ref[:, c0:], wup_ref[...], HI) + bg_ref[...]
    gk = (jnp.minimum(z, 0.0) - jnp.log1p(jnp.exp(-jnp.abs(z)))) * (1.0 / GLA_GATE_NORM)
    b_all = _bmm(tri, gk.reshape(bb, C, GLA_KW), HI).reshape(bb * C, GLA_KW)
    q = pairs(x_ref[:, :GLA_KW], 0, LANES) * (GLA_DK ** -0.5)
    k = pairs(x_ref[:, GLA_KW:2 * GLA_KW], 0, LANES)
    b = pairs(b_all, 0, LANES)
    v = pairs(x_ref[:, 2 * GLA_KW:2 * GLA_KW + GLA_WIDTH], 0, 2 * GLA_DV)
    r = pairs(x_ref[:, 2 * GLA_KW + GLA_WIDTH:c0], 0, 2 * GLA_DV)
    S = s_ref[...].reshape(G, LANES, GLA_DV)

    o_blocks = ([], [])
    for i in range(nb):
        bi, qi, ki = b[:, i * SB:(i + 1) * SB], q[:, i * SB:(i + 1) * SB], k[:, i * SB:(i + 1) * SB]
        att_d = [jnp.zeros((G, SB, C), F32), jnp.zeros((G, SB, C), F32)]
        for j in range(SB):
            e = jnp.exp(jnp.where(t_sb >= j, bi - bi[:, j:j + 1], 0.0))
            term = qi * ki[:, j:j + 1] * e
            for h in range(2):
                col = jnp.sum(jnp.where(hmask[h], term, 0.0), axis=-1, keepdims=True)
                att_d[h] = jnp.where(col_c == i * SB + j, jnp.where(t_sb >= j, col, 0.0), att_d[h])
        if i > 0:
            ri = b[:, i * SB - 1:i * SB]
            qd = qi * jnp.exp(bi - ri)
            kd = k * jnp.exp(jnp.minimum(ri - b, 0.0))
        for h in range(2):
            att = att_d[h]
            if i > 0:
                att = jnp.where(col_c < i * SB, _bmm_nt(jnp.where(hmask[h], qd, 0.0), kd), att)
            o_blocks[h].append(_bmm(att, v[:, :, h * GLA_DV:(h + 1) * GLA_DV]))

    qe = q * jnp.exp(b)
    b_last = b[:, C - 1:C]
    upd = _bmm_tn(k * jnp.exp(b_last - b), v)
    dec = jnp.where(eye, jnp.broadcast_to(jnp.exp(b_last), (G, LANES, LANES)), 0.0)
    s_new = _bmm(dec, S, HI) + jnp.where(row_l < GLA_DK, upd[:, :, :GLA_DV], upd[:, :, GLA_DV:])
    s_ref[...] = s_new.reshape(bb, 2, LANES, GLA_DV)
    for h in range(2):
        o_intra = o_blocks[h][0] if nb == 1 else jnp.concatenate(o_blocks[h], axis=1)
        o = o_intra + _bmm(jnp.where(hmask[h], qe, 0.0), S)
        y = (_rms(o, gn_ref[...]) * _silu(r[:, :, h * GLA_DV:(h + 1) * GLA_DV])).reshape(bb, 2, C, GLA_DV)
        for p in range(2):
            hh = 2 * p + h
            o_ref[:, hh * GLA_DV:(hh + 1) * GLA_DV] = y[:, p].reshape(bb * C, GLA_DV)


def _gla(x, s0, wup, bg, gn, *, B, L, row0, bb):
    C = min(CHUNK, L)
    SB = min(GLA_SUB, C)
    nc = L // C
    rb = bb * C
    blk0 = row0 // rb
    kern = functools.partial(_gla_kernel, C=C, SB=SB, bb=bb)
    return pl.pallas_call(
        kern,
        grid=(B // bb, nc),
        in_specs=[pl.BlockSpec((rb, GLA_IN), lambda b, c: (blk0 + b * nc + c, 0)),
                  pl.BlockSpec((bb, 2, LANES, GLA_DV), lambda b, c: (b, 0, 0, 0)),
                  pl.BlockSpec((LANES, GLA_KW), lambda b, c: (0, 0)),
                  pl.BlockSpec((1, GLA_KW), lambda b, c: (0, 0)),
                  pl.BlockSpec((1, GLA_DV), lambda b, c: (0, 0))],
        out_specs=[pl.BlockSpec((rb, GLA_WIDTH), lambda b, c: (b * nc + c, 0)),
                   pl.BlockSpec((bb, 2, LANES, GLA_DV), lambda b, c: (b, 0, 0, 0))],
        out_shape=[jax.ShapeDtypeStruct((B * L, GLA_WIDTH), F32),
                   jax.ShapeDtypeStruct((B, 2, LANES, GLA_DV), F32)],
        compiler_params=_cparams(("parallel", "arbitrary")),
        name="gla",
    )(x, s0, wup, bg, gn)


def _gdn_kernel(x_ref, s0_ref, cb_ref, cw_ref, alog_ref, dt_ref, gn_ref, o_ref, s_ref, cn_ref, xbuf, *, C, bb):
    c = pl.program_id(1)
    W = 3 * GDN_WIDTH
    P0 = SUBLANES - (CONV_W - 1)
    G = bb * GDN_HEADS

    @pl.when(c == 0)
    def _():
        s_ref[...] = s0_ref[...]
        xbuf[:, P0:SUBLANES, :] = cb_ref[...]

    ri = lax.broadcasted_iota(jnp.int32, (1, C, C), 1)
    ci = lax.broadcasted_iota(jnp.int32, (1, C, C), 2)
    incl = ri >= ci
    strict = ri > ci
    eye = (ri == ci).astype(F32)
    tri = jnp.broadcast_to(incl.astype(F32), (bb, C, C))
    ones = jnp.ones((G, C, C), F32)
    n_sq = max(C.bit_length() - 2, 0)

    def heads(a, off, width):
        parts = [a[:, :, off + h * width:off + (h + 1) * width].reshape(bb, 1, C, width) for h in range(GDN_HEADS)]
        return jnp.concatenate(parts, axis=1).reshape(G, C, width)

    xbuf[:, SUBLANES:SUBLANES + C, :] = x_ref[:, :W].reshape(bb, C, W)
    conv = xbuf[:, P0:P0 + C, :] * cw_ref[0:1, :]
    for j in range(1, CONV_W):
        conv = conv + xbuf[:, P0 + j:P0 + j + C, :] * cw_ref[j:j + 1, :]
    tail = xbuf[:, SUBLANES + C - (CONV_W - 1):SUBLANES + C, :]
    xbuf[:, P0:SUBLANES, :] = tail
    cn_ref[...] = tail
    qkv = _silu(conv)

    small = x_ref[:, W + GDN_WIDTH:].reshape(bb, C, LANES)
    beta_all = jax.nn.sigmoid(small)
    g_all = -jnp.exp(alog_ref[...]) * _softplus(small + dt_ref[...])
    b_all = _bmm(tri, g_all, HI)

    qh = heads(qkv, 0, GDN_DK)
    kh = heads(qkv, GDN_WIDTH, GDN_DK)
    vh = heads(qkv, 2 * GDN_WIDTH, GDN_DV)
    qh = qh * lax.rsqrt(jnp.sum(qh * qh, axis=-1, keepdims=True) + EPS) * (GDN_DK ** -0.5)
    kh = kh * lax.rsqrt(jnp.sum(kh * kh, axis=-1, keepdims=True) + EPS)
    beta = heads(beta_all, 0, 1)
    gcol = heads(g_all, GDN_HEADS, 1)
    bcol = heads(b_all, GDN_HEADS, 1)
    brow = _bmm(ones, jnp.where(ri <= ci, gcol, 0.0), HI)
    decay = jnp.where(incl, jnp.exp(jnp.where(incl, bcol - brow, 0.0)), 0.0)
    kb = kh * beta
    nm = -jnp.where(strict, _bmm3(_split(kb), _split(kh), nt=True) * decay, 0.0)
    tinv = eye + nm
    pw = nm
    for _ in range(n_sq):
        pws = _split(pw)
        pw = _bmm3(pws, pws)
        tinv = tinv + _bmm3(_split(tinv), _split(pw))
    ebc = jnp.exp(bcol)
    tis = _split(tinv)
    u = _bmm3(tis, _split(vh * beta))
    w = _bmm3(tis, _split(kb * ebc))
    S = s_ref[...].reshape(G, GDN_DK, GDN_DV)
    v_new = u - _bmm(w, S)
    qk = _bmm_nt(qh, kh) * decay
    o = _bmm(qh * ebc, S) + _bmm(qk, v_new)
    b_last = bcol[:, C - 1:C]
    s_new = jnp.exp(b_last) * S + _bmm_tn(kh * jnp.exp(b_last - bcol), v_new)
    s_ref[...] = s_new.reshape(bb, GDN_HEADS, GDN_DK, GDN_DV)
    zg = heads(x_ref[:, W:W + GDN_WIDTH].reshape(bb, C, GDN_WIDTH), 0, GDN_DV)
    y = (_rms(o, gn_ref[...]) * _silu(zg)).reshape(bb, GDN_HEADS, C, GDN_DV)
    for h in range(GDN_HEADS):
        o_ref[:, h * GDN_DV:(h + 1) * GDN_DV] = y[:, h].reshape(bb * C, GDN_DV)


def _gdn(x, s0, cbuf, cw, alog, dt, gn, *, B, L, row0, bb):
    C = min(CHUNK, L)
    nc = L // C
    rb = bb * C
    blk0 = row0 // rb
    W = 3 * GDN_WIDTH
    kern = functools.partial(_gdn_kernel, C=C, bb=bb)
    return pl.pallas_call(
        kern,
        grid=(B // bb, nc),
        in_specs=[pl.BlockSpec((rb, GDN_IN), lambda b, c: (blk0 + b * nc + c, 0)),
                  pl.BlockSpec((bb, GDN_HEADS, GDN_DK, GDN_DV), lambda b, c: (b, 0, 0, 0)),
                  pl.BlockSpec((bb, CONV_W - 1, W), lambda b, c: (b, 0, 0)),
                  pl.BlockSpec((CONV_W, W), lambda b, c: (0, 0)),
                  pl.BlockSpec((1, LANES), lambda b, c: (0, 0)),
                  pl.BlockSpec((1, LANES), lambda b, c: (0, 0)),
                  pl.BlockSpec((1, GDN_DV), lambda b, c: (0, 0))],
        out_specs=[pl.BlockSpec((rb, GDN_WIDTH), lambda b, c: (b * nc + c, 0)),
                   pl.BlockSpec((bb, GDN_HEADS, GDN_DK, GDN_DV), lambda b, c: (b, 0, 0, 0)),
                   pl.BlockSpec((bb, CONV_W - 1, W), lambda b, c: (b, 0, 0))],
        out_shape=[jax.ShapeDtypeStruct((B * L, GDN_WIDTH), F32),
                   jax.ShapeDtypeStruct((B, GDN_HEADS, GDN_DK, GDN_DV), F32),
                   jax.ShapeDtypeStruct((B, CONV_W - 1, W), F32)],
        scratch_shapes=[pltpu.VMEM((bb, SUBLANES + C, W), F32)],
        compiler_params=_cparams(("parallel", "arbitrary")),
        name="gdn",
    )(x, s0, cbuf, cw, alog, dt, gn)


def _mid_kernel(h_ref, o1_ref, o2_ref, wo_ref, g_ref, wqh_ref, wql_ref, keys_ref, h1_ref, xn_ref, st_ref):
    h1 = (h_ref[...] + _dot(o1_ref[...].astype(BF16), wo_ref[:GLA_WIDTH, :])
          + _dot(o2_ref[...].astype(BF16), wo_ref[GLA_WIDTH:, :]))
    h1_ref[...] = h1
    xn = _rms(h1, g_ref[...])
    x_hi = xn.astype(BF16)
    xn_ref[...] = x_hi
    x_lo = (xn - x_hi.astype(F32)).astype(BF16)
    q = _dot(x_hi, wqh_ref[...]) + (_dot(x_hi, wql_ref[...]) + _dot(x_lo, wqh_ref[...]))
    for hc in range(2 * PEER_HEADS):
        st_ref[hc] = _dot_nt(keys_ref[hc], q[:, hc * PEER_HALF:(hc + 1) * PEER_HALF], HI)


def _mid(h, o1, o2, wo, g, wqh, wql, keys):
    n = h.shape[0]
    tm = _tile(n, 256)
    nq = 2 * PEER_HEADS * PEER_HALF
    return pl.pallas_call(
        _mid_kernel,
        grid=(n // tm,),
        in_specs=[pl.BlockSpec((tm, D_MODEL), lambda i: (i, 0)),
                  pl.BlockSpec((tm, GLA_WIDTH), lambda i: (i, 0)),
                  pl.BlockSpec((tm, GDN_WIDTH), lambda i: (i, 0)),
                  pl.BlockSpec((D_MODEL, D_MODEL), lambda i: (0, 0)),
                  pl.BlockSpec((1, D_MODEL), lambda i: (0, 0)),
                  pl.BlockSpec((D_MODEL, nq), lambda i: (0, 0)),
                  pl.BlockSpec((D_MODEL, nq), lambda i: (0, 0)),
                  pl.BlockSpec((2 * PEER_HEADS, N_KEYS, PEER_HALF), lambda i: (0, 0, 0))],
        out_specs=[pl.BlockSpec((tm, D_MODEL), lambda i: (i, 0)),
                   pl.BlockSpec((tm, D_MODEL), lambda i: (i, 0)),
                   pl.BlockSpec((2 * PEER_HEADS, N_KEYS, tm), lambda i: (0, 0, i))],
        out_shape=[jax.ShapeDtypeStruct((n, D_MODEL), F32),
                   jax.ShapeDtypeStruct((n, D_MODEL), BF16),
                   jax.ShapeDtypeStruct((2 * PEER_HEADS, N_KEYS, n), F32)],
        compiler_params=_cparams(("parallel",)),
        name="mid",
    )(h, o1, o2, wo, g, wqh, wql, keys)


CAND_ROWS = 80


def _top16(S, io, n):
    vals = []
    for k in range(PEER_TOPK):
        m = jnp.max(S, axis=0, keepdims=True)
        vals.append(m)
        if k + 1 < PEER_TOPK:
            idx = jnp.min(jnp.where(S == m, io, n), axis=0, keepdims=True)
            S = jnp.where(io == idx, -jnp.inf, S)
    return vals


def _topk_kernel(s_ref, e_ref, sel_ref, v_ref, c_ref, *, tm):
    io = lax.broadcasted_iota(jnp.int32, (N_KEYS, tm), 0)
    io_c = lax.broadcasted_iota(jnp.int32, (CAND_ROWS, tm), 0)
    io8 = lax.broadcasted_iota(jnp.int32, (SUBLANES, tm), 0)

    def head(h, carry):
        for c in range(2):
            vals = _top16(s_ref[2 * h + c], io, N_KEYS)
            for k in range(PEER_TOPK):
                v_ref[c, k:k + 1, :] = vals[k]
        c_ref[0:16, :] = v_ref[0, 0:1, :] + v_ref[1, 0:16, :]
        off = 16
        for i in range(1, 8):
            nj = PEER_TOPK // (i + 1)
            c_ref[off:off + SUBLANES, :] = jnp.where(io8 < nj, v_ref[0, i:i + 1, :] + v_ref[1, 0:SUBLANES, :], -jnp.inf)
            off += SUBLANES
        c_ref[off:off + SUBLANES, :] = v_ref[0, 8:16, :] + v_ref[1, 0:1, :]
        cv = _top16(c_ref[...], io_c, CAND_ROWS)
        z = jnp.ones_like(cv[0])
        for k in range(1, PEER_TOPK):
            z = z + jnp.exp(cv[k] - cv[0])
        tau = cv[PEER_TOPK - 1]
        s1 = s_ref[2 * h]
        s2 = s_ref[2 * h + 1]
        theta = jnp.full((N_KEYS, tm), jnp.inf, F32)
        for k in range(PEER_TOPK):
            v2 = v_ref[1, k:k + 1, :]
            theta = jnp.where(s1 + v2 >= tau, v2, theta)
        sel_ref[2 * h] = theta
        sel_ref[2 * h + 1] = s2
        e_ref[2 * h] = jnp.exp(s1 - v_ref[0, 0:1, :]) * (1.0 / z)
        e_ref[2 * h + 1] = jnp.exp(s2 - v_ref[1, 0:1, :])
        return carry

    for h in range(PEER_HEADS):
        head(h, 0)


def _topk(st):
    n = st.shape[-1]
    tm = _tile(n, 128)
    kern = functools.partial(_topk_kernel, tm=tm)
    return pl.pallas_call(
        kern,
        grid=(n // tm,),
        in_specs=[pl.BlockSpec((2 * PEER_HEADS, N_KEYS, tm), lambda i: (0, 0, i))],
        out_specs=[pl.BlockSpec((2 * PEER_HEADS, N_KEYS, tm), lambda i: (0, 0, i)),
                   pl.BlockSpec((2 * PEER_HEADS, N_KEYS, tm), lambda i: (0, 0, i))],
        out_shape=[jax.ShapeDtypeStruct((2 * PEER_HEADS, N_KEYS, n), F32),
                   jax.ShapeDtypeStruct((2 * PEER_HEADS, N_KEYS, n), F32)],
        scratch_shapes=[pltpu.VMEM((2, PEER_TOPK, tm), F32), pltpu.VMEM((CAND_ROWS, tm), F32)],
        compiler_params=_cparams(("parallel",)),
        name="topk",
    )(st)


def _peer_kernel(xn_ref, s_ref, e_ref, u_ref, vt_ref, out_ref, acc_ref, st_ref, p_ref, *, te, tm):
    j = pl.program_id(1)
    n1 = te // N_KEYS

    @pl.when(j == 0)
    def _():
        acc_ref[...] = jnp.zeros_like(acc_ref)

    st_ref[...] = _dot_nt(u_ref[...], xn_ref[...])

    def body(jj, carry):
        i1 = j * n1 + jj
        g = jnp.zeros((N_KEYS, tm), F32)
        for h in range(PEER_HEADS):
            w = e_ref[2 * h, pl.ds(i1, 1), :] * e_ref[2 * h + 1]
            g = g + jnp.where(s_ref[2 * h + 1] >= s_ref[2 * h, pl.ds(i1, 1), :], w, 0.0)
        r = pl.multiple_of(jj * N_KEYS, N_KEYS)
        hid = st_ref[pl.ds(r, N_KEYS), :]
        act = 0.5 * hid * (1.0 + lax.erf(hid * (0.5 ** 0.5)))
        p_ref[pl.ds(r, N_KEYS), :] = (act * g).astype(BF16)
        return carry

    lax.fori_loop(0, n1, body, 0)
    acc_ref[...] += _dot(vt_ref[...], p_ref[...])

    @pl.when(j == pl.num_programs(1) - 1)
    def _():
        out_ref[...] = acc_ref[...].T


def _peer(xn, sel, e, u, vt, te=2048):
    n = xn.shape[0]
    tm = _tile(n, 512)
    kern = functools.partial(_peer_kernel, te=te, tm=tm)
    return pl.pallas_call(
        kern,
        grid=(n // tm, N_EXPERTS // te),
        in_specs=[pl.BlockSpec((tm, D_MODEL), lambda i, j: (i, 0)),
                  pl.BlockSpec((2 * PEER_HEADS, N_KEYS, tm), lambda i, j: (0, 0, i)),
                  pl.BlockSpec((2 * PEER_HEADS, N_KEYS, tm), lambda i, j: (0, 0, i)),
                  pl.BlockSpec((te, D_MODEL), lambda i, j: (j, 0)),
                  pl.BlockSpec((D_MODEL, te), lambda i, j: (0, j))],
        out_specs=pl.BlockSpec((tm, D_MODEL), lambda i, j: (i, 0)),
        out_shape=jax.ShapeDtypeStruct((n, D_MODEL), F32),
        scratch_shapes=[pltpu.VMEM((D_MODEL, tm), F32), pltpu.VMEM((te, tm), F32), pltpu.VMEM((te, tm), BF16)],
        compiler_params=_cparams(("parallel", "arbitrary")),
        name="peer",
    )(xn, sel, e, u, vt)


def _ple_kernel(h1_ref, po_ref, p_ref, g_ref, wg_ref, bg_ref, wp_ref, gf_ref, out_ref, *, final):
    h2 = h1_ref[...] + po_ref[...]
    gate = jax.nn.sigmoid(_dot(_rms(h2, g_ref[...]).astype(BF16), wg_ref[...]) + bg_ref[...])
    h3 = h2 + gate * _dot(p_ref[...].astype(BF16), wp_ref[...])
    out_ref[...] = _rms(h3, gf_ref[...]) if final else h3


def _ple(h1, po, p, g, wg, bg, wp, gf, final):
    n = h1.shape[0]
    tm = _tile(n, 512)
    kern = functools.partial(_ple_kernel, final=final)
    row = lambda i: (i, 0)
    fix = lambda i: (0, 0)
    return pl.pallas_call(
        kern,
        grid=(n // tm,),
        in_specs=[pl.BlockSpec((tm, D_MODEL), row), pl.BlockSpec((tm, D_MODEL), row),
                  pl.BlockSpec((tm, PLE_DIM), row), pl.BlockSpec((1, D_MODEL), fix),
                  pl.BlockSpec((D_MODEL, D_MODEL), fix), pl.BlockSpec((1, D_MODEL), fix),
                  pl.BlockSpec((PLE_DIM, D_MODEL), fix), pl.BlockSpec((1, D_MODEL), fix)],
        out_specs=pl.BlockSpec((tm, D_MODEL), row),
        out_shape=jax.ShapeDtypeStruct((n, D_MODEL), F32),
        compiler_params=_cparams(("parallel",)),
        name="ple",
    )(h1, po, p, g, wg, bg, wp, gf)


def _pad_lanes(x, width=LANES):
    pad = [(0, 0)] * (x.ndim - 1) + [(0, width - x.shape[-1])]
    return jnp.pad(x, pad)


def _trunk(h, p_all, groups, states, prm):
    (norm_mix, w_in_p, wup_p, b_gla_gate, gla_norm, conv_w, alog_row, dt_row, gdn_norm, w_out_b,
     norm_ffn, wq_hi, wq_lo, keys, u_b, vt_b, norm_ple, w_ple_gate_b, b_ple_gate, w_ple_proj_b, norm_final) = prm
    new_states = [([], [], []) for _ in groups]
    depth = w_in_p.shape[0]
    for i in range(depth):
        gla_in, gdn_in = _in_proj(h, norm_mix[i][None], w_in_p[i])
        o1s, o2s = [], []
        for gi, (B, L, row0, bb) in enumerate(groups):
            s_gla, s_gdn, s_conv = states[gi]
            o1, sg = _gla(gla_in, s_gla[i].reshape(B, 2, LANES, GLA_DV), wup_p[i], b_gla_gate[i][None],
                          gla_norm[i][None], B=B, L=L, row0=row0, bb=bb)
            o2, sd, sc = _gdn(gdn_in, s_gdn[i], s_conv[i], conv_w[i], alog_row[i], dt_row[i], gdn_norm[i][None],
                              B=B, L=L, row0=row0, bb=bb)
            o1s.append(o1)
            o2s.append(o2)
            new_states[gi][0].append(sg.reshape(B, GLA_HEADS, GLA_DK, GLA_DV))
            new_states[gi][1].append(sd)
            new_states[gi][2].append(sc)
        o1 = jnp.concatenate(o1s, axis=0)
        o2 = jnp.concatenate(o2s, axis=0)
        h1, xn, st = _mid(h, o1, o2, w_out_b[i], norm_ffn[i][None], wq_hi[i], wq_lo[i], keys[i])
        e, sel = _topk(st)
        po = _peer(xn, sel, e, u_b[i], vt_b[i])
        h = _ple(h1, po, p_all[i], norm_ple[i][None], w_ple_gate_b[i], b_ple_gate[i][None], w_ple_proj_b[i],
                 norm_final[None], final=(i == depth - 1))
    return h, [tuple(jnp.stack(x) for x in ns) for ns in new_states]


def kernel(x_prompt, x_sample, state_gla, state_gdn, state_conv, p_prompt, p_sample, norm_mix, w_in, w_gla_up, b_gla_gate, gla_norm, conv_w, gdn_a_log, gdn_dt_bias, gdn_norm, w_out, norm_ffn, w_peer_q, peer_sub_keys, peer_u, peer_v, norm_ple, w_ple_gate, b_ple_gate, w_ple_proj, norm_final):
    Bp, Lp, _ = x_prompt.shape
    Bs, Ls, _ = x_sample.shape
    n_p, n_s = Bp * Lp, Bs * Ls
    depth = w_in.shape[0]

    c0 = 2 * GLA_KW + 2 * GLA_WIDTH
    c1 = c0 + GLA_RANK
    c2 = c1 + 3 * GDN_WIDTH + GDN_WIDTH
    w_in_p = jnp.concatenate([w_in[..., :c0], _pad_lanes(w_in[..., c0:c1]), w_in[..., c1:c2],
                              _pad_lanes(w_in[..., c2:])], axis=-1).astype(BF16)
    wup_p = jnp.pad(w_gla_up, ((0, 0), (0, LANES - GLA_RANK), (0, 0)))
    lane_pad = ((0, 0), (GDN_HEADS, LANES - 2 * GDN_HEADS))
    alog_row = jnp.pad(gdn_a_log, lane_pad)[:, None, :]
    dt_row = jnp.pad(gdn_dt_bias, lane_pad)[:, None, :]
    wq_hi = w_peer_q.astype(BF16)
    wq_lo = (w_peer_q - wq_hi.astype(F32)).astype(BF16)
    keys = peer_sub_keys.reshape(depth, 2 * PEER_HEADS, N_KEYS, PEER_HALF)
    u_b = peer_u.astype(BF16)
    vt_b = jnp.swapaxes(peer_v, 1, 2).astype(BF16)
    prm = (norm_mix, w_in_p, wup_p, b_gla_gate, gla_norm, conv_w, alog_row, dt_row, gdn_norm, w_out.astype(BF16),
           norm_ffn, wq_hi, wq_lo, keys, u_b, vt_b, norm_ple, w_ple_gate.astype(BF16), b_ple_gate,
           w_ple_proj.astype(BF16), norm_final)

    cp = min(CHUNK, Lp)
    ncp = Lp // cp
    xp = x_prompt.reshape(Bp, ncp, cp, D_MODEL).transpose(1, 0, 2, 3).reshape(n_p, D_MODEL)
    pp = p_prompt.reshape(depth, Bp, ncp, cp, PLE_DIM).transpose(0, 2, 1, 3, 4).reshape(depth, n_p, PLE_DIM)
    h = jnp.concatenate([xp, x_sample.reshape(n_s, D_MODEL)], axis=0)
    p_all = jnp.concatenate([pp, p_sample.reshape(depth, n_s, PLE_DIM)], axis=1)
    zeros = lambda *s: jnp.zeros(s, F32)
    st_p = (zeros(depth, Bp, GLA_HEADS, GLA_DK, GLA_DV), zeros(depth, Bp, GDN_HEADS, GDN_DK, GDN_DV),
            zeros(depth, Bp, CONV_W - 1, 3 * GDN_WIDTH))
    st_s = (state_gla, state_gdn, state_conv)
    groups = [(Bp, Lp, 0, Bp), (Bs, Ls, n_p, min(Bs, SAMPLE_SEQS_PER_STEP))]
    y, (ns_p, ns_s) = _trunk(h, p_all, groups, [st_p, st_s], prm)
    y_prompt = y[:n_p].reshape(ncp, Bp, cp, D_MODEL).transpose(1, 0, 2, 3).reshape(Bp, Lp, D_MODEL)
    y_sample = y[n_p:].reshape(Bs, Ls, D_MODEL)
    return (y_prompt, y_sample, ns_p[0], ns_p[1], ns_p[2], ns_s[0], ns_s[1], ns_s[2])
```
